```python
import math
import jax, jax.numpy as jnp
from jax import lax
import numpy as np

D_MODEL = 1024
BATCH = 8
SEQ = 4096
DEPTH = 2

SC_GROUPS = 8
SC_WIDTH = 512
SC_WIDTH_CONV = 3
CF_GROUPS = 8
CF_WIDTH = 512
CF_WIDTH_CONV = 31
EV_IN = 3 * SC_WIDTH + 2 * CF_WIDTH
EV_OUT_IN = SC_WIDTH + CF_WIDTH

LRU_HEADS = 8
LRU_WIDTH = 512
LRU_BLOCK = LRU_WIDTH // LRU_HEADS
LRU_CONV = 4
LRU_C = 8.0
DIL_PATTERNS = ((128, 1), (512, 4), (2048, 16))
N_DIL_GROUPS = len(DIL_PATTERNS)
DIL_HEADS = 4
DIL_HEAD_DIM = 64
DIL_OUT = DIL_HEADS * DIL_HEAD_DIM
DIL_QKV = N_DIL_GROUPS * DIL_OUT
ATTN_BLOCK = 128
OD_IN = 2 * LRU_WIDTH + 3 * DIL_QKV
OD_OUT_IN = LRU_WIDTH + DIL_OUT

D_FF = 2816
FFN_CONV = 3

N_EVEN = (DEPTH + 1) // 2
N_ODD = DEPTH // 2
EPS = 1e-6

kernel_name = "hybrid_conv_lru_dilated_trunk"


def rmsnorm(x, g):
    xf = x.astype(jnp.float32)
    y = xf * lax.rsqrt(jnp.mean(xf * xf, axis=-1, keepdims=True) + EPS)
    return (y * g.astype(jnp.float32)).astype(x.dtype)


def layernorm(x, g, b):
    xf = x.astype(jnp.float32)
    mu = jnp.mean(xf, axis=-1, keepdims=True)
    var = jnp.mean(jnp.square(xf - mu), axis=-1, keepdims=True)
    y = (xf - mu) * lax.rsqrt(var + EPS) * g.astype(jnp.float32) + b.astype(jnp.float32)
    return y.astype(x.dtype)


def causal_dwconv(x, w):
    K, C = w.shape
    return lax.conv_general_dilated(
        x, w[:, None, :].astype(x.dtype), window_strides=(1,), padding=[(K - 1, 0)],
        dimension_numbers=("NWC", "WIO", "NWC"), feature_group_count=C)


def even_mixer(h, w_in, sc_conv_w, cf_conv_w, cf_conv_b, cf_ln_g, cf_ln_b, w_out):
    z = h @ w_in
    sc_b, sc_c, sc_x, cf_a, cf_g = jnp.split(
        z, [SC_WIDTH, 2 * SC_WIDTH, 3 * SC_WIDTH, 3 * SC_WIDTH + CF_WIDTH], axis=-1)
    y_sc = sc_b * causal_dwconv(sc_c * sc_x, sc_conv_w)
    u = cf_a * jax.nn.sigmoid(cf_g)
    u = causal_dwconv(u, cf_conv_w) + cf_conv_b
    u = jax.nn.silu(layernorm(u, cf_ln_g, cf_ln_b))
    return jnp.concatenate([y_sc, u], axis=-1) @ w_out


def rg_lru(xc, w_a, b_a, w_x, b_x, lam):
    B, S, W = xc.shape
    xb = xc.reshape(B, S, LRU_HEADS, LRU_BLOCK)
    r = jax.nn.sigmoid(jnp.einsum("bshi,hij->bshj", xb, w_a).reshape(B, S, W) + b_a)
    i = jax.nn.sigmoid(jnp.einsum("bshi,hij->bshj", xb, w_x).reshape(B, S, W) + b_x)
    log_a = -LRU_C * r.astype(jnp.float32) * jax.nn.softplus(-lam.astype(jnp.float32))
    a = jnp.exp(log_a)
    mult = jnp.sqrt(-jnp.expm1(2.0 * log_a))
    bt = mult * (i * xc).astype(jnp.float32)

    def combine(left, right):
        a1, b1 = left
        a2, b2 = right
        return a1 * a2, a2 * b1 + b2

    _, hs = lax.associative_scan(combine, (a, bt), axis=1)
    return hs.astype(xc.dtype)


def dilated_group_attention(q, k, v, window, dilation):
    B, S, H, Dh = q.shape
    L = S // dilation
    nb = -(-L // ATTN_BLOCK)
    Lp = nb * ATTN_BLOCK
    span = window // dilation

    def strided(t):
        t = t.reshape(B, L, dilation, H, Dh).transpose(0, 2, 1, 3, 4)
        t = jnp.pad(t, ((0, 0), (0, 0), (0, Lp - L), (0, 0), (0, 0)))
        return t.reshape(B, dilation, nb, ATTN_BLOCK, H, Dh)

    def with_prev(t):
        prev = jnp.pad(t[:, :, :-1], ((0, 0), (0, 0), (1, 0), (0, 0), (0, 0), (0, 0)))
        return jnp.concatenate([prev, t], axis=3)

    qb = strided(q)
    kk = with_prev(strided(k))
    vv = with_prev(strided(v))
    s = jnp.einsum("brnqhd,brnkhd->brnhqk", qb, kk).astype(jnp.float32)
    qi = jnp.arange(ATTN_BLOCK)[:, None]
    kj = jnp.arange(2 * ATTN_BLOCK)[None, :]
    steps = ATTN_BLOCK + qi - kj
    band = (steps >= 0) & (steps <= span)
    blk_idx = jnp.arange(nb)[:, None, None]
    mask = band[None] & ((blk_idx > 0) | (kj >= ATTN_BLOCK)[None])
    s = jnp.where(mask[None, None, :, None], s, -jnp.inf)
    m = jnp.max(s, axis=-1, keepdims=True)
    p = jnp.exp(s - m)
    den = jnp.sum(p, axis=-1, keepdims=True)
    o = jnp.einsum("brnhqk,brnkhd->brnqhd", p, vv.astype(jnp.float32))
    o = o / jnp.swapaxes(den, 3, 4)
    lse = jnp.swapaxes((m + jnp.log(den))[..., 0], 3, 4)
    o = o.reshape(B, dilation, Lp, H, Dh)[:, :, :L].transpose(0, 2, 1, 3, 4).reshape(B, S, H, Dh)
    lse = lse.reshape(B, dilation, Lp, H)[:, :, :L].transpose(0, 2, 1, 3).reshape(B, S, H)
    return o, lse


def dilated_attention(q, k, v, q_norm_g, k_norm_g):
    B, S, _ = q.shape
    shp = (B, S, N_DIL_GROUPS, DIL_HEADS, DIL_HEAD_DIM)
    qn = rmsnorm(q.reshape(shp), q_norm_g) * (DIL_HEAD_DIM ** -0.5)
    kn = rmsnorm(k.reshape(shp), k_norm_g)
    vr = v.reshape(shp)
    outs, lses = [], []
    for g, (window, dilation) in enumerate(DIL_PATTERNS):
        o, lse = dilated_group_attention(qn[:, :, g], kn[:, :, g], vr[:, :, g], window, dilation)
        outs.append(o)
        lses.append(lse)
    o = jnp.stack(outs)
    wts = jax.nn.softmax(jnp.stack(lses), axis=0)
    out = jnp.einsum("gbsh,gbshd->bshd", wts, o)
    return out.reshape(B, S, DIL_OUT).astype(q.dtype)


def odd_mixer(h, w_in, lru_conv_w, lru_conv_b, lru_wa, lru_ba, lru_wx, lru_bx, lru_lam,
              q_norm_g, k_norm_g, w_out):
    z = h @ w_in
    lru_x, lru_gate, q, k, v = jnp.split(
        z, [LRU_WIDTH, 2 * LRU_WIDTH, 2 * LRU_WIDTH + DIL_QKV, 2 * LRU_WIDTH + 2 * DIL_QKV], axis=-1)
    xc = causal_dwconv(lru_x, lru_conv_w) + lru_conv_b
    y_lru = rg_lru(xc, lru_wa, lru_ba, lru_wx, lru_bx, lru_lam) * jax.nn.gelu(lru_gate)
    y_att = dilated_attention(q, k, v, q_norm_g, k_norm_g)
    return jnp.concatenate([y_lru, y_att], axis=-1) @ w_out


def conv_glu_ffn(h, w_gate, w_up, conv_w, conv_b, w_down):
    g = causal_dwconv(h @ w_gate, conv_w) + conv_b
    return (jax.nn.silu(g) * (h @ w_up)) @ w_down


def setup_inputs(seed: int = 0) -> dict:
    key = jax.random.key(seed)
    ks = iter(jax.random.split(key, 40))
    f32 = jnp.float32

    def nrm(shape, fan_in):
        return jax.random.normal(next(ks), shape, f32) * (fan_in ** -0.5)

    def gain(shape):
        return 1.0 + 0.05 * jax.random.normal(next(ks), shape, f32)

    def bias(shape):
        return 0.01 * jax.random.normal(next(ks), shape, f32)

    u = jax.random.uniform(next(ks), (N_ODD, LRU_WIDTH), f32, minval=0.9, maxval=0.999)
    a0 = u ** (1.0 / LRU_C)
    lam = jnp.log(a0) - jnp.log1p(-a0)

    return {
        "x": jax.random.normal(next(ks), (BATCH, SEQ, D_MODEL), f32),
        "mix_norm_g": gain((DEPTH, D_MODEL)),
        "ffn_norm_g": gain((DEPTH, D_MODEL)),
        "ev_w_in": nrm((N_EVEN, D_MODEL, EV_IN), D_MODEL),
        "ev_sc_conv_w": nrm((N_EVEN, SC_WIDTH_CONV, SC_WIDTH), SC_WIDTH_CONV),
        "ev_cf_conv_w": nrm((N_EVEN, CF_WIDTH_CONV, CF_WIDTH), CF_WIDTH_CONV),
        "ev_cf_conv_b": bias((N_EVEN, CF_WIDTH)),
        "ev_cf_ln_g": gain((N_EVEN, CF_WIDTH)),
        "ev_cf_ln_b": bias((N_EVEN, CF_WIDTH)),
        "ev_w_out": nrm((N_EVEN, EV_OUT_IN, D_MODEL), EV_OUT_IN),
        "od_w_in": nrm((N_ODD, D_MODEL, OD_IN), D_MODEL),
        "od_lru_conv_w": nrm((N_ODD, LRU_CONV, LRU_WIDTH), LRU_CONV),
        "od_lru_conv_b": bias((N_ODD, LRU_WIDTH)),
        "od_lru_wa": nrm((N_ODD, LRU_HEADS, LRU_BLOCK, LRU_BLOCK), LRU_BLOCK),
        "od_lru_ba": bias((N_ODD, LRU_WIDTH)),
        "od_lru_wx": nrm((N_ODD, LRU_HEADS, LRU_BLOCK, LRU_BLOCK), LRU_BLOCK),
        "od_lru_bx": bias((N_ODD, LRU_WIDTH)),
        "od_lru_lam": lam,
        "od_q_norm_g": gain((N_ODD, DIL_HEAD_DIM)),
        "od_k_norm_g": gain((N_ODD, DIL_HEAD_DIM)),
        "od_w_out": nrm((N_ODD, OD_OUT_IN, D_MODEL), OD_OUT_IN),
        "ffn_w_gate": nrm((DEPTH, D_MODEL, D_FF), D_MODEL),
        "ffn_w_up": nrm((DEPTH, D_MODEL, D_FF), D_MODEL),
        "ffn_conv_w": nrm((DEPTH, FFN_CONV, D_FF), FFN_CONV),
        "ffn_conv_b": bias((DEPTH, D_FF)),
        "ffn_w_down": nrm((DEPTH, D_FF, D_MODEL), D_FF),
    }


def reference(x, mix_norm_g, ffn_norm_g,
              ev_w_in, ev_sc_conv_w, ev_cf_conv_w, ev_cf_conv_b, ev_cf_ln_g, ev_cf_ln_b, ev_w_out,
              od_w_in, od_lru_conv_w, od_lru_conv_b, od_lru_wa, od_lru_ba, od_lru_wx, od_lru_bx,
              od_lru_lam, od_q_norm_g, od_k_norm_g, od_w_out,
              ffn_w_gate, ffn_w_up, ffn_conv_w, ffn_conv_b, ffn_w_down):
    for layer in range(DEPTH):
        j = layer // 2
        h = rmsnorm(x, mix_norm_g[layer])
        if layer % 2 == 0:
            x = x + even_mixer(h, ev_w_in[j], ev_sc_conv_w[j], ev_cf_conv_w[j], ev_cf_conv_b[j],
                               ev_cf_ln_g[j], ev_cf_ln_b[j], ev_w_out[j])
        else:
            x = x + odd_mixer(h, od_w_in[j], od_lru_conv_w[j], od_lru_conv_b[j], od_lru_wa[j],
                              od_lru_ba[j], od_lru_wx[j], od_lru_bx[j], od_lru_lam[j],
                              od_q_norm_g[j], od_k_norm_g[j], od_w_out[j])
        h = rmsnorm(x, ffn_norm_g[layer])
        x = x + conv_glu_ffn(h, ffn_w_gate[layer], ffn_w_up[layer], ffn_conv_w[layer],
                             ffn_conv_b[layer], ffn_w_down[layer])
    return x
```

```python
import functools
import math

import jax
import jax.numpy as jnp
from jax import lax
from jax.experimental import pallas as pl
from jax.experimental.pallas import tpu as pltpu

F32 = jnp.float32
BF16 = jnp.bfloat16

D_MODEL = 1024
SC_WIDTH = 512
CF_WIDTH = 512
CF_TAPS = 31
LRU_WIDTH = 512
LRU_TAPS = 4
LRU_C = 8.0
DIL_PATTERNS = ((128, 1), (512, 4), (2048, 16))
DIL_HEADS = 4
DIL_HEAD_DIM = 64
DIL_OUT = DIL_HEADS * DIL_HEAD_DIM
DIL_QKV = len(DIL_PATTERNS) * DIL_OUT
ATTN_BLOCK = 128
D_FF = 2816
EPS = 1e-6

SUBLANES = 8
LANES = 128
SEQ_TILE = 512
CF_HALO = 32
ROW_CHUNK = 32
FFN_COL_CHUNK = 256
VMEM_LIMIT = 56 * 1024 * 1024


def _rms_bf16(x, g):
    y = x * lax.rsqrt(jnp.mean(x * x, axis=-1, keepdims=True) + EPS)
    return (y * g).astype(BF16)


def _sigmoid(x):
    return 1.0 / (1.0 + jnp.exp(-x))


def _rows_back(win, s, n):
    if s == 0:
        return win[SUBLANES:SUBLANES + n]
    return pltpu.roll(win, s, 0)[SUBLANES:SUBLANES + n]


def _causal_taps(win, w_ref, n_taps, n):
    acc = w_ref[n_taps - 1:n_taps, :] * _rows_back(win, 0, n)
    for k in range(n_taps - 1):
        acc += w_ref[k:k + 1, :] * _rows_back(win, n_taps - 1 - k, n)
    return acc


def _const_spec(shape):
    return pl.BlockSpec(shape, lambda *_: (0,) * len(shape), pipeline_mode=pl.Buffered(1))


def _tile_spec(width, ts):
    return pl.BlockSpec((1, ts, width), lambda b, j: (b, j, 0))


def _lane_split_spec(width, ts):
    return pl.BlockSpec((1, width // LANES, ts, LANES), lambda b, j: (b, 0, j, 0))


def _even_kernel(x_ref, g_ref, win_ref, scw_ref, cfw_ref, cfb_ref, lng_ref, lnb_ref, wout_ref,
                 o_ref, z_buf, cx_buf, u_buf, ush_buf, cat_buf, *, ts):
    @pl.when(pl.program_id(1) == 0)
    def _():
        cx_buf[0:SUBLANES, :] = jnp.zeros((SUBLANES, SC_WIDTH), F32)
        u_buf[0:CF_HALO, :] = jnp.zeros((CF_HALO, CF_WIDTH), F32)
        u_buf[ts + CF_HALO:ts + CF_HALO + SUBLANES, :] = jnp.zeros((SUBLANES, CF_WIDTH), F32)

    h = _rms_bf16(x_ref[0], g_ref[...])
    z_buf[...] = jnp.dot(h, win_ref[...], preferred_element_type=F32)

    n_chunks = ts // ROW_CHUNK

    def stage(i, c):
        r = pl.multiple_of(i * ROW_CHUNK, ROW_CHUNK)
        rows = pl.ds(r, ROW_CHUNK)
        cx_buf[pl.ds(r + SUBLANES, ROW_CHUNK), :] = z_buf[rows, 512:1024] * z_buf[rows, 1024:1536]
        u_buf[pl.ds(r + CF_HALO, ROW_CHUNK), :] = z_buf[rows, 1536:2048] * _sigmoid(z_buf[rows, 2048:2560])
        return c

    lax.fori_loop(0, n_chunks, stage, 0)

    def shift_copies(i, c):
        r = pl.multiple_of(i * ROW_CHUNK, ROW_CHUNK)
        win = u_buf[pl.ds(r, ROW_CHUNK + SUBLANES), :]
        for s in range(1, SUBLANES):
            ush_buf[s - 1, pl.ds(r, ROW_CHUNK), :] = pltpu.roll(win, ROW_CHUNK + SUBLANES - s, 0)[0:ROW_CHUNK]
        return c

    lax.fori_loop(0, (ts + CF_HALO) // ROW_CHUNK, shift_copies, 0)

    def mix(i, c):
        r = pl.multiple_of(i * ROW_CHUNK, ROW_CHUNK)
        rows = pl.ds(r, ROW_CHUNK)
        acc = _causal_taps(cx_buf[pl.ds(r, ROW_CHUNK + SUBLANES), :], scw_ref, 3, ROW_CHUNK)
        cat_buf[rows, 0:SC_WIDTH] = (z_buf[rows, 0:512] * acc).astype(BF16)
        u = None
        for k in range(CF_TAPS):
            off = CF_HALO - (CF_TAPS - 1) + k
            s = off % SUBLANES
            src_rows = pl.ds(pl.multiple_of(r + (off - s), SUBLANES), ROW_CHUNK)
            src = u_buf[src_rows, :] if s == 0 else ush_buf[s - 1, src_rows, :]
            term = cfw_ref[k:k + 1, :] * src
            u = term if u is None else u + term
        u += cfb_ref[...]
        mu = jnp.mean(u, axis=-1, keepdims=True)
        uc = u - mu
        var = jnp.mean(uc * uc, axis=-1, keepdims=True)
        y = uc * lax.rsqrt(var + EPS) * lng_ref[...] + lnb_ref[...]
        cat_buf[rows, SC_WIDTH:SC_WIDTH + CF_WIDTH] = (y * _sigmoid(y)).astype(BF16)
        return c

    lax.fori_loop(0, n_chunks, mix, 0)

    o_ref[0] = x_ref[0] + jnp.dot(cat_buf[...], wout_ref[...], preferred_element_type=F32)
    cx_buf[0:SUBLANES, :] = cx_buf[ts:ts + SUBLANES, :]
    u_buf[0:CF_HALO, :] = u_buf[ts:ts + CF_HALO, :]


def _even_mixer(x, g, w_in, sc_w, cf_w, cf_b, ln_g, ln_b, w_out, ts=SEQ_TILE):
    B, S, D = x.shape
    ev_in = w_in.shape[1]
    return pl.pallas_call(
        functools.partial(_even_kernel, ts=ts),
        grid=(B, S // ts),
        in_specs=[
            _tile_spec(D, ts),
            _const_spec((1, D)),
            _const_spec((D, ev_in)),
            _const_spec(sc_w.shape),
            _const_spec(cf_w.shape),
            _const_spec((1, CF_WIDTH)),
            _const_spec((1, CF_WIDTH)),
            _const_spec((1, CF_WIDTH)),
            _const_spec(w_out.shape),
        ],
        out_specs=_tile_spec(D, ts),
        out_shape=jax.ShapeDtypeStruct((B, S, D), F32),
        scratch_shapes=[
            pltpu.VMEM((ts, ev_in), F32),
            pltpu.VMEM((ts + SUBLANES, SC_WIDTH), F32),
            pltpu.VMEM((ts + CF_HALO + SUBLANES, CF_WIDTH), F32),
            pltpu.VMEM((SUBLANES - 1, ts + CF_HALO, CF_WIDTH), F32),
            pltpu.VMEM((ts, SC_WIDTH + CF_WIDTH), BF16),
        ],
        compiler_params=pltpu.CompilerParams(
            dimension_semantics=("arbitrary", "arbitrary"), vmem_limit_bytes=VMEM_LIMIT),
        name="even_mixer",
    )(x, g, w_in, sc_w, cf_w, cf_b, ln_g, ln_b, w_out)


def _ffn_body(x, ng_ref, wg_ref, wu_ref, cw_ref, cb_ref, wd_ref, o_ref, g_buf, up_buf, act_buf, ts):
    @pl.when(pl.program_id(1) == 0)
    def _():
        g_buf[0:SUBLANES, :] = jnp.zeros((SUBLANES, D_FF), F32)

    h = _rms_bf16(x, ng_ref[...])
    g_buf[SUBLANES:SUBLANES + ts, :] = jnp.dot(h, wg_ref[...], preferred_element_type=F32)
    up_buf[...] = jnp.dot(h, wu_ref[...], preferred_element_type=F32)

    def act(i, c):
        r = pl.multiple_of(i * ROW_CHUNK, ROW_CHUNK)
        rows = pl.ds(r, ROW_CHUNK)
        for c0 in range(0, D_FF, FFN_COL_CHUNK):
            cols = slice(c0, c0 + FFN_COL_CHUNK)
            gc = _causal_taps(g_buf[pl.ds(r, ROW_CHUNK + SUBLANES), cols], cw_ref.at[:, cols], 3, ROW_CHUNK)
            gc += cb_ref[:, cols]
            act_buf[rows, cols] = (gc * _sigmoid(gc) * up_buf[rows, cols]).astype(BF16)
        return c

    lax.fori_loop(0, ts // ROW_CHUNK, act, 0)
    o_ref[0] = x + jnp.dot(act_buf[...], wd_ref[...], preferred_element_type=F32)
    g_buf[0:SUBLANES, :] = g_buf[ts:ts + SUBLANES, :]


def _ffn_kernel(x_ref, ng_ref, wg_ref, wu_ref, cw_ref, cb_ref, wd_ref, o_ref,
                g_buf, up_buf, act_buf, *, ts):
    _ffn_body(x_ref[0], ng_ref, wg_ref, wu_ref, cw_ref, cb_ref, wd_ref, o_ref, g_buf, up_buf, act_buf, ts)


def _proj_ffn_kernel(x_ref, ya_ref, yb_ref, wo_ref, ng_ref, wg_ref, wu_ref, cw_ref, cb_ref, wd_ref, o_ref,
                     g_buf, up_buf, act_buf, *, ts):
    wa = ya_ref.shape[-1]
    yb = jnp.concatenate([yb_ref[0, c] for c in range(yb_ref.shape[1])], axis=1)
    x = x_ref[0] + jnp.dot(ya_ref[0].astype(BF16), wo_ref[0:wa, :], preferred_element_type=F32)
    x = x + jnp.dot(yb.astype(BF16), wo_ref[wa:, :], preferred_element_type=F32)
    _ffn_body(x, ng_ref, wg_ref, wu_ref, cw_ref, cb_ref, wd_ref, o_ref, g_buf, up_buf, act_buf, ts)


def _conv_ffn(x, ng, wg, wu, cw, cb, wd, mix=None, ts=SEQ_TILE):
    B, S, D = x.shape
    ffn_specs = [
        _const_spec((1, D)),
        _const_spec(wg.shape),
        _const_spec(wu.shape),
        _const_spec(cw.shape),
        _const_spec((1, D_FF)),
        _const_spec(wd.shape),
    ]
    if mix is None:
        body, ins, specs = _ffn_kernel, (x,), [_tile_spec(D, ts)]
    else:
        ya, yb, wo = mix
        body, ins = _proj_ffn_kernel, (x, ya, yb, wo)
        specs = [_tile_spec(D, ts), _tile_spec(ya.shape[-1], ts), _lane_split_spec(yb.shape[1] * LANES, ts),
                 _const_spec(wo.shape)]
    return pl.pallas_call(
        functools.partial(body, ts=ts),
        grid=(B, S // ts),
        in_specs=specs + ffn_specs,
        out_specs=_tile_spec(D, ts),
        out_shape=jax.ShapeDtypeStruct((B, S, D), F32),
        scratch_shapes=[
            pltpu.VMEM((ts + SUBLANES, D_FF), F32),
            pltpu.VMEM((ts, D_FF), F32),
            pltpu.VMEM((ts, D_FF), BF16),
        ],
        compiler_params=pltpu.CompilerParams(
            dimension_semantics=("arbitrary", "arbitrary"), vmem_limit_bytes=VMEM_LIMIT),
        name="conv_ffn" if mix is None else "proj_conv_ffn",
    )(*ins, ng, wg, wu, cw, cb, wd)


def _odd_in_kernel(x_ref, g_ref, win_ref, cw_ref, cb_ref, wa_ref, ba_ref, wx_ref, bx_ref, lam_ref,
                   qg_ref, kg_ref, seg_ref,
                   ylru_ref, q_ref, k_ref, v_ref,
                   z_buf, xin_buf, xc_buf, xcb_buf, ga_buf, gx_buf, h_carry, *, ts):
    @pl.when(pl.program_id(1) == 0)
    def _():
        xin_buf[0:SUBLANES, :] = jnp.zeros((SUBLANES, LRU_WIDTH), F32)
        h_carry[...] = jnp.zeros((SUBLANES, LRU_WIDTH), F32)

    h = _rms_bf16(x_ref[0], g_ref[...])
    z_buf[...] = jnp.dot(h, win_ref[...], preferred_element_type=F32)

    q0, k0, v0 = 2 * LRU_WIDTH, 2 * LRU_WIDTH + DIL_QKV, 2 * LRU_WIDTH + 2 * DIL_QKV

    def qk_norm(i, c):
        r = pl.multiple_of(i * ROW_CHUNK * 4, ROW_CHUNK * 4)
        rows = pl.ds(r, ROW_CHUNK * 4)
        for col, gain_ref, out_ref, scale in ((q0, qg_ref, q_ref, DIL_HEAD_DIM ** -0.5), (k0, kg_ref, k_ref, 1.0)):
            t = z_buf[rows, col:col + DIL_QKV]
            ss = jnp.dot((t * t).astype(BF16), seg_ref[...], preferred_element_type=F32)
            t = t * lax.rsqrt(ss * (1.0 / DIL_HEAD_DIM) + EPS) * (gain_ref[...] * scale)
            for c0 in range(DIL_QKV // LANES):
                out_ref[0, c0, rows, :] = t[:, c0 * LANES:(c0 + 1) * LANES]
        for c0 in range(DIL_QKV // LANES):
            v_ref[0, c0, rows, :] = z_buf[rows, v0 + c0 * LANES:v0 + (c0 + 1) * LANES]
        return c

    lax.fori_loop(0, ts // (ROW_CHUNK * 4), qk_norm, 0)

    xin_buf[SUBLANES:SUBLANES + ts, :] = z_buf[:, 0:LRU_WIDTH]

    def conv(i, c):
        r = pl.multiple_of(i * ROW_CHUNK, ROW_CHUNK)
        acc = _causal_taps(xin_buf[pl.ds(r, ROW_CHUNK + SUBLANES), :], cw_ref, LRU_TAPS, ROW_CHUNK)
        acc += cb_ref[...]
        xc_buf[pl.ds(r, ROW_CHUNK), :] = acc
        xcb_buf[pl.ds(r, ROW_CHUNK), :] = acc.astype(BF16)
        return c

    lax.fori_loop(0, ts // ROW_CHUNK, conv, 0)
    xin_buf[0:SUBLANES, :] = xin_buf[ts:ts + SUBLANES, :]

    ga_buf[...] = jnp.dot(xcb_buf[...], wa_ref[...], preferred_element_type=F32)
    gx_buf[...] = jnp.dot(xcb_buf[...], wx_ref[...], preferred_element_type=F32)

    nlam = -lam_ref[...]
    softplus = jnp.maximum(nlam, 0.0) + jnp.log(1.0 + jnp.exp(-jnp.abs(nlam)))
    decay = jnp.broadcast_to(-LRU_C * softplus, (SUBLANES, LRU_WIDTH))
    ba = jnp.broadcast_to(ba_ref[...], (SUBLANES, LRU_WIDTH))
    bx = jnp.broadcast_to(bx_ref[...], (SUBLANES, LRU_WIDTH))
    row = lax.broadcasted_iota(jnp.int32, (SUBLANES, LRU_WIDTH), 0)

    def scan(i, hc):
        r = pl.multiple_of(i * SUBLANES, SUBLANES)
        rows = pl.ds(r, SUBLANES)
        xc = xc_buf[rows, :]
        rg = _sigmoid(ga_buf[rows, :] + ba)
        ig = _sigmoid(gx_buf[rows, :] + bx)
        log_a = decay * rg
        a = jnp.exp(log_a)
        b = jnp.sqrt(1.0 - a * a) * (ig * xc)
        for s in (1, 2, 4):
            a_prev = jnp.where(row >= s, pltpu.roll(a, s, 0), 1.0)
            b_prev = jnp.where(row >= s, pltpu.roll(b, s, 0), 0.0)
            b = a * b_prev + b
            a = a * a_prev
        hs = a * hc + b
        gate = z_buf[rows, LRU_WIDTH:2 * LRU_WIDTH]
        gelu = 0.5 * gate * (1.0 + jnp.tanh(math.sqrt(2.0 / math.pi) * (gate + 0.044715 * (gate * gate * gate))))
        ylru_ref[0, rows, :] = hs * gelu
        return jnp.broadcast_to(hs[SUBLANES - 1:SUBLANES, :], (SUBLANES, LRU_WIDTH))

    h_carry[...] = lax.fori_loop(0, ts // SUBLANES, scan, h_carry[...])


def _odd_in(x, g, w_in, cw, cb, wa, ba, wx, bx, lam, qg, kg, seg, ts=SEQ_TILE):
    B, S, D = x.shape
    od_in = w_in.shape[1]
    return pl.pallas_call(
        functools.partial(_odd_in_kernel, ts=ts),
        grid=(B, S // ts),
        in_specs=[
            _tile_spec(D, ts),
            _const_spec((1, D)),
            _const_spec(w_in.shape),
            _const_spec(cw.shape),
            _const_spec((1, LRU_WIDTH)),
            _const_spec(wa.shape),
            _const_spec((1, LRU_WIDTH)),
            _const_spec(wx.shape),
            _const_spec((1, LRU_WIDTH)),
            _const_spec((1, LRU_WIDTH)),
            _const_spec((1, DIL_QKV)),
            _const_spec((1, DIL_QKV)),
            _const_spec(seg.shape),
        ],
        out_specs=[_tile_spec(LRU_WIDTH, ts)] + [_lane_split_spec(DIL_QKV, ts)] * 3,
        out_shape=[jax.ShapeDtypeStruct((B, S, LRU_WIDTH), F32)]
        + [jax.ShapeDtypeStruct((B, DIL_QKV // LANES, S, LANES), F32)] * 3,
        scratch_shapes=[
            pltpu.VMEM((ts, od_in), F32),
            pltpu.VMEM((ts + SUBLANES, LRU_WIDTH), F32),
            pltpu.VMEM((ts, LRU_WIDTH), F32),
            pltpu.VMEM((ts, LRU_WIDTH), BF16),
            pltpu.VMEM((ts, LRU_WIDTH), F32),
            pltpu.VMEM((ts, LRU_WIDTH), F32),
            pltpu.VMEM((SUBLANES, LRU_WIDTH), F32),
        ],
        compiler_params=pltpu.CompilerParams(
            dimension_semantics=("arbitrary", "arbitrary"), vmem_limit_bytes=VMEM_LIMIT),
        name="odd_in_lru",
    )(x, g, w_in, cw, cb, wa, ba, wx, bx, lam, qg, kg, seg)


def _attn_group(q_ref, k_ref, v_ref, o_ref, m_run, l_run, acc, *, group, dilation, seq):
    blk = ATTN_BLOCK
    nb = seq // dilation // blk
    n_split = DIL_OUT // LANES
    lane_head = lax.broadcasted_iota(jnp.int32, (1, DIL_OUT), 1) // DIL_HEAD_DIM
    head_mask = [(lane_head == h).astype(F32) for h in range(DIL_HEADS)]
    qi = lax.broadcasted_iota(jnp.int32, (DIL_HEADS * blk, 2 * blk), 0) % blk
    kj = lax.broadcasted_iota(jnp.int32, (DIL_HEADS * blk, 2 * blk), 1)
    rel = qi - kj

    def rows_of(start, n):
        if dilation == 1:
            return pl.ds(pl.multiple_of(start, blk), n)
        return pl.ds(start, n, stride=dilation)

    def load(ref, rows, lead=()):
        return jnp.concatenate([ref[lead + (c, rows, slice(None))] for c in range(n_split)], axis=1)

    def store(ref, rows, val, lead=()):
        for c in range(n_split):
            ref[lead + (c, rows, slice(None))] = val[:, c * LANES:(c + 1) * LANES]

    def body(idx, carry):
        res = idx // nb
        b = idx % nb
        kb = jnp.maximum(b - 1, 0)
        q_rows = rows_of(res + dilation * blk * b, blk)
        k_rows = rows_of(res + dilation * blk * kb, 2 * blk)
        qb = load(q_ref, q_rows, (0,))
        qs = jnp.concatenate([qb * head_mask[h] for h in range(DIL_HEADS)], axis=0).astype(BF16)
        kw = load(k_ref, k_rows, (0,)).astype(BF16)
        vw = load(v_ref, k_rows, (0,)).astype(BF16)
        s = lax.dot_general(qs, kw, (((1,), (1,)), ((), ())), preferred_element_type=F32)
        steps = rel + (b - kb) * blk
        s = jnp.where((steps >= 0) & (steps <= blk), s, -jnp.inf)
        m = jnp.max(s, axis=-1, keepdims=True)
        p = jnp.exp(s - m)
        l = jnp.sum(p, axis=-1, keepdims=True)
        o = jnp.dot(p.astype(BF16), vw, preferred_element_type=F32)
        o_tok = o[0:blk] * head_mask[0]
        m_tok = m[0:blk] * head_mask[0]
        l_tok = l[0:blk] * head_mask[0]
        for h in range(1, DIL_HEADS):
            o_tok += o[h * blk:(h + 1) * blk] * head_mask[h]
            m_tok += m[h * blk:(h + 1) * blk] * head_mask[h]
            l_tok += l[h * blk:(h + 1) * blk] * head_mask[h]
        if group == 0:
            m_new, l_new, acc_new = m_tok, l_tok, o_tok
        else:
            m_old = load(m_run, q_rows)
            m_new = jnp.maximum(m_old, m_tok)
            alpha = jnp.exp(m_old - m_new)
            beta = jnp.exp(m_tok - m_new)
            l_new = load(l_run, q_rows) * alpha + l_tok * beta
            acc_new = load(acc, q_rows) * alpha + o_tok * beta
        if group == len(DIL_PATTERNS) - 1:
            store(o_ref, q_rows, acc_new / l_new, (0,))
        else:
            store(m_run, q_rows, m_new)
            store(l_run, q_rows, l_new)
            store(acc, q_rows, acc_new)
        return carry

    lax.fori_loop(0, dilation * nb, body, 0)


def _attn_kernel(q_ref, k_ref, v_ref, o_ref, m_run, l_run, acc, *, seq):
    g = pl.program_id(1)
    for group, (window, dilation) in enumerate(DIL_PATTERNS):
        assert window // dilation == ATTN_BLOCK

        @pl.when(g == group)
        def _(group=group, dilation=dilation):
            _attn_group(q_ref, k_ref, v_ref, o_ref, m_run, l_run, acc,
                        group=group, dilation=dilation, seq=seq)


def _dilated_attention(q, k, v):
    B, _, S, _ = q.shape
    n_split = DIL_OUT // LANES
    qkv_spec = pl.BlockSpec((1, n_split, S, LANES), lambda b, g: (b, g, 0, 0))
    return pl.pallas_call(
        functools.partial(_attn_kernel, seq=S),
        grid=(B, len(DIL_PATTERNS)),
        in_specs=[qkv_spec, qkv_spec, qkv_spec],
        out_specs=pl.BlockSpec((1, n_split, S, LANES), lambda b, g: (b, 0, 0, 0)),
        out_shape=jax.ShapeDtypeStruct((B, n_split, S, LANES), F32),
        scratch_shapes=[pltpu.VMEM((n_split, S, LANES), F32)] * 3,
        compiler_params=pltpu.CompilerParams(
            dimension_semantics=("arbitrary", "arbitrary"), vmem_limit_bytes=VMEM_LIMIT),
        name="dilated_attention",
    )(q, k, v)


def _block_diag(w):
    H, n, _ = w.shape
    eye = jnp.eye(H, dtype=w.dtype)
    return (eye[:, None, :, None] * w[:, :, None, :]).reshape(H * n, H * n)


def kernel(x, mix_norm_g, ffn_norm_g, ev_w_in, ev_sc_conv_w, ev_cf_conv_w, ev_cf_conv_b, ev_cf_ln_g, ev_cf_ln_b, ev_w_out, od_w_in, od_lru_conv_w, od_lru_conv_b, od_lru_wa, od_lru_ba, od_lru_wx, od_lru_bx, od_lru_lam, od_q_norm_g, od_k_norm_g, od_w_out, ffn_w_gate, ffn_w_up, ffn_conv_w, ffn_conv_b, ffn_w_down):
    row = lambda p: p.reshape(1, -1)
    bf = lambda w: w.astype(BF16)
    n_heads = DIL_QKV // DIL_HEAD_DIM
    head_of = jnp.arange(DIL_QKV) // DIL_HEAD_DIM
    seg = (head_of[:, None] == head_of[None, :]).astype(BF16)

    def ffn(x, layer, mix=None):
        return _conv_ffn(x, row(ffn_norm_g[layer]), bf(ffn_w_gate[layer]), bf(ffn_w_up[layer]),
                         ffn_conv_w[layer], row(ffn_conv_b[layer]), bf(ffn_w_down[layer]), mix=mix)

    x = _even_mixer(x, row(mix_norm_g[0]), bf(ev_w_in[0]), ev_sc_conv_w[0], ev_cf_conv_w[0],
                    row(ev_cf_conv_b[0]), row(ev_cf_ln_g[0]), row(ev_cf_ln_b[0]), bf(ev_w_out[0]))
    x = ffn(x, 0)
    y_lru, q, k, v = _odd_in(
        x, row(mix_norm_g[1]), bf(od_w_in[0]), od_lru_conv_w[0], row(od_lru_conv_b[0]),
        bf(_block_diag(od_lru_wa[0])), row(od_lru_ba[0]), bf(_block_diag(od_lru_wx[0])), row(od_lru_bx[0]),
        row(od_lru_lam[0]), row(jnp.tile(od_q_norm_g[0], n_heads)), row(jnp.tile(od_k_norm_g[0], n_heads)), seg)
    y_att = _dilated_attention(q, k, v)
    return ffn(x, 1, mix=(y_lru, y_att, bf(od_w_out[0])))
```

```python
import functools
import math

import jax
import jax.numpy as jnp
from jax import lax
from jax.experimental import pallas as pl
from jax.experimental.pallas import tpu as pltpu

F32 = jnp.float32
BF16 = jnp.bfloat16

D_MODEL = 1024
SC_WIDTH = 512
SC_TAPS = 3
CF_WIDTH = 512
CF_TAPS = 31
LRU_WIDTH = 512
LRU_TAPS = 4
LRU_C = 8.0
GATE_BLOCK = 256
DIL_PATTERNS = ((128, 1), (512, 4), (2048, 16))
DIL_HEADS = 4
DIL_HEAD_DIM = 64
DIL_OUT = DIL_HEADS * DIL_HEAD_DIM
DIL_QKV = len(DIL_PATTERNS) * DIL_OUT
ATTN_BLOCK = 128
D_FF = 2816
EPS = 1e-6

SUBLANES = 8
LANES = 128
SEQ_TILE = 512
CF_HALO = 32
ROW_CHUNK = 32
FFN_COL_CHUNK = 256
VMEM_LIMIT = 56 * 1024 * 1024


def _rms_bf16(x, g):
    y = x * lax.rsqrt(jnp.mean(x * x, axis=-1, keepdims=True) + EPS)
    return (y * g).astype(BF16)


def _sigmoid(x):
    return 1.0 / (1.0 + jnp.exp(-x))


def _rep8(p):
    return jnp.repeat(p.reshape(-1, p.shape[-1]), SUBLANES, axis=0)


def _bcast_rows(ref, k, cols, n):
    w = ref[k * SUBLANES:(k + 1) * SUBLANES, cols]
    return jnp.concatenate([w] * (n // SUBLANES), axis=0)


def _load_lanes(ref, start, n):
    return jnp.concatenate([ref[c, start:start + n, :] for c in range(ref.shape[0])], axis=1)


def _store_lanes(ref, start, val):
    for c in range(ref.shape[0]):
        ref[c, start:start + val.shape[0], :] = val[:, c * LANES:(c + 1) * LANES]


def _const_spec(shape):
    return pl.BlockSpec(shape, lambda *_: (0,) * len(shape), pipeline_mode=pl.Buffered(1))


def _tile_spec(width, ts):
    return pl.BlockSpec((1, ts, width), lambda b, j: (b, j, 0))


def _lane_split_spec(width, ts):
    return pl.BlockSpec((1, width // LANES, ts, LANES), lambda b, j: (b, 0, j, 0))


def _even_kernel(x_ref, g_ref, win_ref, scw_ref, cfw_ref, cfb_ref, lng_ref, lnb_ref, wout_ref,
                 o_ref, h_buf, z_buf, cx_buf, u_buf, cat_buf, *, ts):
    @pl.when(pl.program_id(1) == 0)
    def _():
        cx_buf[:, 0:SUBLANES, :] = jnp.zeros((SC_WIDTH // LANES, SUBLANES, LANES), F32)
        u_buf[:, 0:CF_HALO, :] = jnp.zeros((CF_WIDTH // LANES, CF_HALO, LANES), F32)

    sc_b, sc_c, sc_x, cf_a, cf_g = range(5)
    width = SC_WIDTH
    all_cols = slice(None)
    h_buf[...] = _rms_bf16(x_ref[0], g_ref[...])

    def proj(c):
        z_buf[c] = jnp.dot(h_buf[...], win_ref[:, c * width:(c + 1) * width], preferred_element_type=F32)

    proj(cf_a)
    proj(cf_g)
    for r in range(0, ts, ROW_CHUNK):
        rows = slice(r, r + ROW_CHUNK)
        _store_lanes(u_buf, CF_HALO + r, z_buf[cf_a, rows, :] * _sigmoid(z_buf[cf_g, rows, :]))
    proj(sc_c)
    proj(sc_x)
    for r in range(0, ts, ROW_CHUNK):
        rows = slice(r, r + ROW_CHUNK)
        _store_lanes(cx_buf, SUBLANES + r, z_buf[sc_c, rows, :] * z_buf[sc_x, rows, :])
    proj(sc_b)

    n_parts = 2
    for part in range(n_parts):
        part_rows = slice(part * ts // n_parts, (part + 1) * ts // n_parts)
        for r in range(part_rows.start, part_rows.stop, ROW_CHUNK):
            rows = slice(r, r + ROW_CHUNK)
            acc = None
            for k in range(SC_TAPS):
                term = (_bcast_rows(scw_ref, k, all_cols, ROW_CHUNK)
                        * _load_lanes(cx_buf, SUBLANES + r - (SC_TAPS - 1) + k, ROW_CHUNK))
                acc = term if acc is None else acc + term
            cat_buf[rows, 0:SC_WIDTH] = (z_buf[sc_b, rows, :] * acc).astype(BF16)
            u = None
            for k in range(CF_TAPS):
                term = (_bcast_rows(cfw_ref, k, all_cols, ROW_CHUNK)
                        * _load_lanes(u_buf, CF_HALO + r - (CF_TAPS - 1) + k, ROW_CHUNK))
                u = term if u is None else u + term
            u += _bcast_rows(cfb_ref, 0, all_cols, ROW_CHUNK)
            mu = jnp.mean(u, axis=-1, keepdims=True)
            uc = u - mu
            var = jnp.mean(uc * uc, axis=-1, keepdims=True)
            y = (uc * lax.rsqrt(var + EPS) * _bcast_rows(lng_ref, 0, all_cols, ROW_CHUNK)
                 + _bcast_rows(lnb_ref, 0, all_cols, ROW_CHUNK))
            cat_buf[rows, SC_WIDTH:SC_WIDTH + CF_WIDTH] = (y * _sigmoid(y)).astype(BF16)
        o_ref[0, part_rows, :] = x_ref[0, part_rows, :] + jnp.dot(
            cat_buf[part_rows, :], wout_ref[...], preferred_element_type=F32)

    cx_buf[:, 0:SUBLANES, :] = cx_buf[:, ts:ts + SUBLANES, :]
    u_buf[:, 0:CF_HALO, :] = u_buf[:, ts:ts + CF_HALO, :]


def _even_mixer(x, g, w_in, sc_w, cf_w, cf_b, ln_g, ln_b, w_out, ts=SEQ_TILE):
    B, S, D = x.shape
    ev_in = w_in.shape[1]
    return pl.pallas_call(
        functools.partial(_even_kernel, ts=ts),
        grid=(B, S // ts),
        in_specs=[
            _tile_spec(D, ts),
            _const_spec((1, D)),
            _const_spec((D, ev_in)),
            _const_spec(sc_w.shape),
            _const_spec(cf_w.shape),
            _const_spec(cf_b.shape),
            _const_spec(ln_g.shape),
            _const_spec(ln_b.shape),
            _const_spec(w_out.shape),
        ],
        out_specs=_tile_spec(D, ts),
        out_shape=jax.ShapeDtypeStruct((B, S, D), F32),
        scratch_shapes=[
            pltpu.VMEM((ts, D), BF16),
            pltpu.VMEM((ev_in // SC_WIDTH, ts, SC_WIDTH), F32),
            pltpu.VMEM((SC_WIDTH // LANES, ts + SUBLANES, LANES), F32),
            pltpu.VMEM((CF_WIDTH // LANES, ts + CF_HALO, LANES), F32),
            pltpu.VMEM((ts, SC_WIDTH + CF_WIDTH), BF16),
        ],
        compiler_params=pltpu.CompilerParams(
            dimension_semantics=("arbitrary", "arbitrary"), vmem_limit_bytes=VMEM_LIMIT),
        name="even_mixer",
    )(x, g, w_in, sc_w, cf_w, cf_b, ln_g, ln_b, w_out)


def _ffn_body(x, ng_ref, wg_ref, wu_ref, cw_ref, cb_ref, wd_ref, o_ref, h_buf, g_buf, up_buf, act_buf, ts):
    @pl.when(pl.program_id(1) == 0)
    def _():
        g_buf[:, 0:SUBLANES, :] = jnp.zeros((D_FF // LANES, SUBLANES, LANES), F32)

    def g_rows(c0, start):
        return jnp.concatenate([g_buf[(c0 + l) // LANES, start:start + ROW_CHUNK, :]
                                for l in range(0, FFN_COL_CHUNK, LANES)], axis=1)

    h_buf[...] = _rms_bf16(x, ng_ref[...])
    for c0 in range(0, D_FF, FFN_COL_CHUNK):
        cols = slice(c0, c0 + FFN_COL_CHUNK)
        g = jnp.dot(h_buf[...], wg_ref[:, cols], preferred_element_type=F32)
        for l in range(0, FFN_COL_CHUNK, LANES):
            g_buf[(c0 + l) // LANES, SUBLANES:SUBLANES + ts, :] = g[:, l:l + LANES]
        up_buf[:, cols] = jnp.dot(h_buf[...], wu_ref[:, cols], preferred_element_type=F32)
        w0, w1, w2 = (_bcast_rows(cw_ref, k, cols, ROW_CHUNK) for k in range(3))
        bias = _bcast_rows(cb_ref, 0, cols, ROW_CHUNK)
        for r in range(0, ts, ROW_CHUNK):
            gc = (w0 * g_rows(c0, r + SUBLANES - 2) + w1 * g_rows(c0, r + SUBLANES - 1)
                  + w2 * g_rows(c0, r + SUBLANES) + bias)
            act_buf[r:r + ROW_CHUNK, cols] = (gc * _sigmoid(gc) * up_buf[r:r + ROW_CHUNK, cols]).astype(BF16)
    o_ref[0] = x + jnp.dot(act_buf[...], wd_ref[...], preferred_element_type=F32)
    g_buf[:, 0:SUBLANES, :] = g_buf[:, ts:ts + SUBLANES, :]


def _ffn_kernel(x_ref, ng_ref, wg_ref, wu_ref, cw_ref, cb_ref, wd_ref, o_ref,
                *scratch, ts):
    _ffn_body(x_ref[0], ng_ref, wg_ref, wu_ref, cw_ref, cb_ref, wd_ref, o_ref, *scratch, ts)


def _proj_ffn_kernel(x_ref, ya_ref, yb_ref, wo_ref, ng_ref, wg_ref, wu_ref, cw_ref, cb_ref, wd_ref, o_ref,
                     *scratch, ts):
    wa = ya_ref.shape[-1]
    yb = jnp.concatenate([yb_ref[0, c] for c in range(yb_ref.shape[1])], axis=1)
    x = x_ref[0] + jnp.dot(ya_ref[0].astype(BF16), wo_ref[0:wa, :], preferred_element_type=F32)
    x = x + jnp.dot(yb.astype(BF16), wo_ref[wa:, :], preferred_element_type=F32)
    _ffn_body(x, ng_ref, wg_ref, wu_ref, cw_ref, cb_ref, wd_ref, o_ref, *scratch, ts)


def _conv_ffn(x, ng, wg, wu, cw, cb, wd, mix=None, ts=SEQ_TILE):
    B, S, D = x.shape
    ffn_specs = [
        _const_spec((1, D)),
        _const_spec(wg.shape),
        _const_spec(wu.shape),
        _const_spec(cw.shape),
        _const_spec(cb.shape),
        _const_spec(wd.shape),
    ]
    if mix is None:
        body, ins, specs = _ffn_kernel, (x,), [_tile_spec(D, ts)]
    else:
        ya, yb, wo = mix
        body, ins = _proj_ffn_kernel, (x, ya, yb, wo)
        specs = [_tile_spec(D, ts), _tile_spec(ya.shape[-1], ts), _lane_split_spec(yb.shape[1] * LANES, ts),
                 _const_spec(wo.shape)]
    return pl.pallas_call(
        functools.partial(body, ts=ts),
        grid=(B, S // ts),
        in_specs=specs + ffn_specs,
        out_specs=_tile_spec(D, ts),
        out_shape=jax.ShapeDtypeStruct((B, S, D), F32),
        scratch_shapes=[
            pltpu.VMEM((ts, D), BF16),
            pltpu.VMEM((D_FF // LANES, ts + SUBLANES, LANES), F32),
            pltpu.VMEM((ts, D_FF), F32),
            pltpu.VMEM((ts, D_FF), BF16),
        ],
        compiler_params=pltpu.CompilerParams(
            dimension_semantics=("arbitrary", "arbitrary"), vmem_limit_bytes=VMEM_LIMIT),
        name="conv_ffn" if mix is None else "proj_conv_ffn",
    )(*ins, ng, wg, wu, cw, cb, wd)


def _odd_in_kernel(x_ref, g_ref, win_ref, cw_ref, cb_ref, wa_ref, ba_ref, wx_ref, bx_ref, lam_ref,
                   qg_ref, kg_ref, seg_ref,
                   ylru_ref, q_ref, k_ref, v_ref,
                   h_buf, xin_buf, xc_buf, xcb_buf, ga_buf, gx_buf, gate_buf, t_buf, a_buf, b_buf, h_carry, *, ts):
    @pl.when(pl.program_id(1) == 0)
    def _():
        xin_buf[:, 0:SUBLANES, :] = jnp.zeros((LRU_WIDTH // LANES, SUBLANES, LANES), F32)
        h_carry[...] = jnp.zeros((SUBLANES, LRU_WIDTH), F32)

    all_cols = slice(None)
    h_buf[...] = _rms_bf16(x_ref[0], g_ref[...])

    def proj(c0, width):
        return jnp.dot(h_buf[...], win_ref[:, c0:c0 + width], preferred_element_type=F32)

    _store_lanes(xin_buf, SUBLANES, proj(0, LRU_WIDTH))
    for r in range(0, ts, ROW_CHUNK):
        acc = _bcast_rows(cb_ref, 0, all_cols, ROW_CHUNK)
        for k in range(LRU_TAPS):
            acc += (_bcast_rows(cw_ref, k, all_cols, ROW_CHUNK)
                    * _load_lanes(xin_buf, SUBLANES + r - (LRU_TAPS - 1) + k, ROW_CHUNK))
        xc_buf[r:r + ROW_CHUNK, :] = acc
        xcb_buf[r:r + ROW_CHUNK, :] = acc.astype(BF16)
    xin_buf[:, 0:SUBLANES, :] = xin_buf[:, ts:ts + SUBLANES, :]

    for w_ref, buf in ((wa_ref, ga_buf), (wx_ref, gx_buf)):
        for c0 in range(0, LRU_WIDTH, GATE_BLOCK):
            cols = slice(c0, c0 + GATE_BLOCK)
            buf[:, cols] = jnp.dot(xcb_buf[:, cols], w_ref[cols, cols], preferred_element_type=F32)
    gate_buf[...] = proj(LRU_WIDTH, LRU_WIDTH)

    nlam = -lam_ref[...]
    softplus = jnp.maximum(nlam, 0.0) + jnp.log(1.0 + jnp.exp(-jnp.abs(nlam)))
    decay = jnp.broadcast_to(-LRU_C * softplus, (SUBLANES, LRU_WIDTH))
    row = lax.broadcasted_iota(jnp.int32, (SUBLANES, LRU_WIDTH), 0)
    for r in range(0, ts, SUBLANES):
        rows = slice(r, r + SUBLANES)
        rg = _sigmoid(ga_buf[rows, :] + ba_ref[...])
        ig = _sigmoid(gx_buf[rows, :] + bx_ref[...])
        a = jnp.exp(decay * rg)
        b = jnp.sqrt(1.0 - a * a) * (ig * xc_buf[rows, :])
        for s in (1, 2, 4):
            a_prev = jnp.where(row >= s, pltpu.roll(a, s, 0), 1.0)
            b_prev = jnp.where(row >= s, pltpu.roll(b, s, 0), 0.0)
            b = a * b_prev + b
            a = a * a_prev
        a_buf[rows, :] = a
        b_buf[rows, :] = b
    hc = h_carry[...]
    for r in range(0, ts, SUBLANES):
        rows = slice(r, r + SUBLANES)
        hs = a_buf[rows, :] * hc + b_buf[rows, :]
        gate = gate_buf[rows, :]
        gelu = 0.5 * gate * (1.0 + jnp.tanh(math.sqrt(2.0 / math.pi) * (gate + 0.044715 * (gate * gate * gate))))
        ylru_ref[0, rows, :] = hs * gelu
        hc = jnp.broadcast_to(hs[SUBLANES - 1:SUBLANES, :], (SUBLANES, LRU_WIDTH))
    h_carry[...] = hc

    q0 = 2 * LRU_WIDTH
    qk_rows = 4 * ROW_CHUNK
    for c0, gain_ref, out_ref, scale in ((q0, qg_ref, q_ref, DIL_HEAD_DIM ** -0.5 * math.log2(math.e)),
                                         (q0 + DIL_QKV, kg_ref, k_ref, 1.0)):
        t_buf[...] = proj(c0, DIL_QKV)
        for r in range(0, ts, qk_rows):
            for g0 in range(0, DIL_QKV, DIL_OUT):
                t = t_buf[r:r + qk_rows, g0:g0 + DIL_OUT]
                ss = jnp.dot((t * t).astype(BF16), seg_ref[...], preferred_element_type=F32)
                gain = _bcast_rows(gain_ref, 0, slice(g0, g0 + DIL_OUT), qk_rows) * scale
                t = t * lax.rsqrt(ss * (1.0 / DIL_HEAD_DIM) + EPS) * gain
                for l in range(0, DIL_OUT, LANES):
                    out_ref[0, (g0 + l) // LANES, r:r + qk_rows, :] = t[:, l:l + LANES]
    v = proj(q0 + 2 * DIL_QKV, DIL_QKV)
    for c in range(DIL_QKV // LANES):
        v_ref[0, c] = v[:, c * LANES:(c + 1) * LANES]


def _odd_in(x, g, w_in, cw, cb, wa, ba, wx, bx, lam, qg, kg, seg, ts=SEQ_TILE):
    B, S, D = x.shape
    od_in = w_in.shape[1]
    return pl.pallas_call(
        functools.partial(_odd_in_kernel, ts=ts),
        grid=(B, S // ts),
        in_specs=[
            _tile_spec(D, ts),
            _const_spec((1, D)),
            _const_spec(w_in.shape),
            _const_spec(cw.shape),
            _const_spec(cb.shape),
            _const_spec(wa.shape),
            _const_spec(ba.shape),
            _const_spec(wx.shape),
            _const_spec(bx.shape),
            _const_spec((1, LRU_WIDTH)),
            _const_spec(qg.shape),
            _const_spec(kg.shape),
            _const_spec(seg.shape),
        ],
        out_specs=[_tile_spec(LRU_WIDTH, ts)] + [_lane_split_spec(DIL_QKV, ts)] * 3,
        out_shape=[jax.ShapeDtypeStruct((B, S, LRU_WIDTH), F32)]
        + [jax.ShapeDtypeStruct((B, DIL_QKV // LANES, S, LANES), F32)] * 3,
        scratch_shapes=[
            pltpu.VMEM((ts, D), BF16),
            pltpu.VMEM((LRU_WIDTH // LANES, ts + SUBLANES, LANES), F32),
            pltpu.VMEM((ts, LRU_WIDTH), F32),
            pltpu.VMEM((ts, LRU_WIDTH), BF16),
            pltpu.VMEM((ts, LRU_WIDTH), F32),
            pltpu.VMEM((ts, LRU_WIDTH), F32),
            pltpu.VMEM((ts, LRU_WIDTH), F32),
            pltpu.VMEM((ts, DIL_QKV), F32),
            pltpu.VMEM((ts, LRU_WIDTH), F32),
            pltpu.VMEM((ts, LRU_WIDTH), F32),
            pltpu.VMEM((SUBLANES, LRU_WIDTH), F32),
        ],
        compiler_params=pltpu.CompilerParams(
            dimension_semantics=("arbitrary", "arbitrary"), vmem_limit_bytes=VMEM_LIMIT),
        name="odd_in_lru",
    )(x, g, w_in, cw, cb, wa, ba, wx, bx, lam, qg, kg, seg)


def _attn_group(q_ref, k_ref, v_ref, o_ref, m_run, l_run, acc, bias_buf, *, group, dilation, seq):
    blk = ATTN_BLOCK
    nb = seq // dilation // blk
    n_split = DIL_OUT // LANES
    lane_head = lax.broadcasted_iota(jnp.int32, (1, DIL_OUT), 1) // DIL_HEAD_DIM
    is_head = [lane_head == h for h in range(DIL_HEADS)]
    head_mask = [m.astype(F32) for m in is_head]

    def fold(t):
        out = t[(DIL_HEADS - 1) * blk:DIL_HEADS * blk]
        for h in range(DIL_HEADS - 2, -1, -1):
            out = jnp.where(is_head[h], t[h * blk:(h + 1) * blk], out)
        return out

    def rows_of(start, n):
        if dilation == 1:
            return pl.ds(pl.multiple_of(start, blk), n)
        return pl.ds(start, n, stride=dilation)

    def load(ref, rows, lead=()):
        return jnp.concatenate([ref[lead + (c, rows, slice(None))] for c in range(n_split)], axis=1)

    def store(ref, rows, val, lead=()):
        for c in range(n_split):
            ref[lead + (c, rows, slice(None))] = val[:, c * LANES:(c + 1) * LANES]

    def body(idx, carry):
        res = idx // nb
        b = idx % nb
        kb = jnp.maximum(b - 1, 0)
        q_rows = rows_of(res + dilation * blk * b, blk)
        k_rows = rows_of(res + dilation * blk * kb, 2 * blk)
        qb = load(q_ref, q_rows, (0,))
        qs = jnp.concatenate([qb * head_mask[h] for h in range(DIL_HEADS)], axis=0).astype(BF16)
        kw = load(k_ref, k_rows, (0,)).astype(BF16)
        vw = load(v_ref, k_rows, (0,)).astype(BF16)
        s = lax.dot_general(qs, kw, (((1,), (1,)), ((), ())), preferred_element_type=F32)
        s = s + bias_buf[jnp.minimum(b, 1)]
        m = jnp.max(s, axis=-1, keepdims=True)
        p = jnp.exp2(s - m)
        l = jnp.sum(p, axis=-1, keepdims=True)
        o = jnp.dot(p.astype(BF16), vw, preferred_element_type=F32)
        o_tok, m_tok, l_tok = fold(o), fold(m), fold(l)
        if group == 0:
            m_new, l_new, acc_new = m_tok, l_tok, o_tok
        else:
            m_old = load(m_run, q_rows)
            m_new = jnp.maximum(m_old, m_tok)
            alpha = jnp.exp2(m_old - m_new)
            beta = jnp.exp2(m_tok - m_new)
            l_new = load(l_run, q_rows) * alpha + l_tok * beta
            acc_new = load(acc, q_rows) * alpha + o_tok * beta
        if group == len(DIL_PATTERNS) - 1:
            store(o_ref, q_rows, acc_new / l_new, (0,))
        else:
            store(m_run, q_rows, m_new)
            store(l_run, q_rows, l_new)
            store(acc, q_rows, acc_new)
        return carry

    lax.fori_loop(0, dilation * nb, body, 0, unroll=4)


def _attn_kernel(q_ref, k_ref, v_ref, o_ref, m_run, l_run, acc, bias_buf, *, seq):
    blk = ATTN_BLOCK
    qi = lax.broadcasted_iota(jnp.int32, (DIL_HEADS * blk, 2 * blk), 0) % blk
    kj = lax.broadcasted_iota(jnp.int32, (DIL_HEADS * blk, 2 * blk), 1)
    for t in range(2):
        steps = qi - kj + t * blk
        bias_buf[t] = jnp.where((steps >= 0) & (steps <= blk), 0.0, -jnp.inf).astype(F32)

    g = pl.program_id(1)
    for group, (window, dilation) in enumerate(DIL_PATTERNS):
        assert window // dilation == ATTN_BLOCK

        @pl.when(g == group)
        def _(group=group, dilation=dilation):
            _attn_group(q_ref, k_ref, v_ref, o_ref, m_run, l_run, acc, bias_buf,
                        group=group, dilation=dilation, seq=seq)


def _dilated_attention(q, k, v):
    B, _, S, _ = q.shape
    n_split = DIL_OUT // LANES
    qkv_spec = pl.BlockSpec((1, n_split, S, LANES), lambda b, g: (b, g, 0, 0))
    return pl.pallas_call(
        functools.partial(_attn_kernel, seq=S),
        grid=(B, len(DIL_PATTERNS)),
        in_specs=[qkv_spec, qkv_spec, qkv_spec],
        out_specs=pl.BlockSpec((1, n_split, S, LANES), lambda b, g: (b, 0, 0, 0)),
        out_shape=jax.ShapeDtypeStruct((B, n_split, S, LANES), F32),
        scratch_shapes=[pltpu.VMEM((n_split, S, LANES), F32)] * 3
        + [pltpu.VMEM((2, DIL_HEADS * ATTN_BLOCK, 2 * ATTN_BLOCK), F32)],
        compiler_params=pltpu.CompilerParams(
            dimension_semantics=("arbitrary", "arbitrary"), vmem_limit_bytes=VMEM_LIMIT),
        name="dilated_attention",
    )(q, k, v)


def _block_diag(w):
    H, n, _ = w.shape
    eye = jnp.eye(H, dtype=w.dtype)
    return (eye[:, None, :, None] * w[:, :, None, :]).reshape(H * n, H * n)


def kernel(x, mix_norm_g, ffn_norm_g, ev_w_in, ev_sc_conv_w, ev_cf_conv_w, ev_cf_conv_b, ev_cf_ln_g, ev_cf_ln_b, ev_w_out, od_w_in, od_lru_conv_w, od_lru_conv_b, od_lru_wa, od_lru_ba, od_lru_wx, od_lru_bx, od_lru_lam, od_q_norm_g, od_k_norm_g, od_w_out, ffn_w_gate, ffn_w_up, ffn_conv_w, ffn_conv_b, ffn_w_down):
    row = lambda p: p.reshape(1, -1)
    bf = lambda w: w.astype(BF16)
    n_heads = DIL_QKV // DIL_HEAD_DIM
    head_of = jnp.arange(DIL_OUT) // DIL_HEAD_DIM
    seg = (head_of[:, None] == head_of[None, :]).astype(BF16)

    def ffn(x, layer, mix=None):
        return _conv_ffn(x, row(ffn_norm_g[layer]), bf(ffn_w_gate[layer]), bf(ffn_w_up[layer]),
                         _rep8(ffn_conv_w[layer]), _rep8(ffn_conv_b[layer]), bf(ffn_w_down[layer]), mix=mix)

    x = _even_mixer(x, row(mix_norm_g[0]), bf(ev_w_in[0]), _rep8(ev_sc_conv_w[0]), _rep8(ev_cf_conv_w[0]),
                    _rep8(ev_cf_conv_b[0]), _rep8(ev_cf_ln_g[0]), _rep8(ev_cf_ln_b[0]), bf(ev_w_out[0]))
    x = ffn(x, 0)
    y_lru, q, k, v = _odd_in(
        x, row(mix_norm_g[1]), bf(od_w_in[0]), _rep8(od_lru_conv_w[0]), _rep8(od_lru_conv_b[0]),
        bf(_block_diag(od_lru_wa[0])), _rep8(od_lru_ba[0]), bf(_block_diag(od_lru_wx[0])), _rep8(od_lru_bx[0]),
        row(od_lru_lam[0]), _rep8(jnp.tile(od_q_norm_g[0], n_heads)), _rep8(jnp.tile(od_k_norm_g[0], n_heads)), seg)
    y_att = _dilated_attention(q, k, v)
    return ffn(x, 1, mix=(y_lru, y_att, bf(od_w_out[0])))
```

```python
import functools
import math

import jax
import jax.numpy as jnp
from jax import lax
from jax.experimental import pallas as pl
from jax.experimental.pallas import tpu as pltpu

F32 = jnp.float32
BF16 = jnp.bfloat16

D_MODEL = 1024
SC_WIDTH = 512
SC_TAPS = 3
CF_WIDTH = 512
CF_TAPS = 31
LRU_WIDTH = 512
LRU_TAPS = 4
LRU_C = 8.0
GATE_BLOCK = 256
DIL_PATTERNS = ((128, 1), (512, 4), (2048, 16))
DIL_HEADS = 4
DIL_HEAD_DIM = 64
DIL_OUT = DIL_HEADS * DIL_HEAD_DIM
DIL_QKV = len(DIL_PATTERNS) * DIL_OUT
ATTN_BLOCK = 128
D_FF = 2816
EPS = 1e-6

SUBLANES = 8
LANES = 128
SEQ_TILE = 512
CF_HALO = 32
ROW_CHUNK = 32
FFN_COL_CHUNK = 256
VMEM_LIMIT = 56 * 1024 * 1024


def _rms_bf16(x, g):
    y = x * lax.rsqrt(jnp.mean(x * x, axis=-1, keepdims=True) + EPS)
    return (y * g).astype(BF16)


def _sigmoid(x):
    return 1.0 / (1.0 + jnp.exp(-x))


def _rep8(p):
    return jnp.repeat(p.reshape(-1, p.shape[-1]), SUBLANES, axis=0)


def _bcast_rows(ref, k, cols, n):
    w = ref[k * SUBLANES:(k + 1) * SUBLANES, cols]
    return jnp.concatenate([w] * (n // SUBLANES), axis=0)


def _schedule_after(val, width):
    bits = lax.bitcast_convert_type(val[-SUBLANES:, -LANES:].astype(F32), jnp.int32)
    zero = lax.shift_right_logical(lax.shift_right_logical(bits, 16), 16).astype(F32)
    return jnp.concatenate([zero] * (width // LANES), axis=1)


def _load_lanes(ref, start, n):
    return jnp.concatenate([ref[c, start:start + n, :] for c in range(ref.shape[0])], axis=1)


def _lane_conv(ref, w_ref, n_taps, start, n):
    out = []
    for c in range(ref.shape[0]):
        cols = slice(c * LANES, (c + 1) * LANES)
        acc = None
        for k in range(n_taps):
            term = _bcast_rows(w_ref, k, cols, n) * ref[c, start + k:start + k + n, :]
            acc = term if acc is None else acc + term
        out.append(acc)
    return jnp.concatenate(out, axis=1)


def _store_lanes(ref, start, val):
    for c in range(ref.shape[0]):
        ref[c, start:start + val.shape[0], :] = val[:, c * LANES:(c + 1) * LANES]


def _const_spec(shape):
    return pl.BlockSpec(shape, lambda *_: (0,) * len(shape), pipeline_mode=pl.Buffered(1))


def _tile_spec(width, ts):
    return pl.BlockSpec((1, ts, width), lambda b, j: (b, j, 0))


def _lane_split_spec(width, ts):
    return pl.BlockSpec((1, width // LANES, ts, LANES), lambda b, j: (b, 0, j, 0))


def _even_kernel(x_ref, xp_ref, g_ref, win_ref, scw_ref, cfw_ref, cfb_ref, lng_ref, lnb_ref, wout_ref,
                 o_ref, h_buf, z_buf, zb_buf, cx_buf, u_buf, cat_buf, *, ts, nt):
    i = pl.program_id(0)
    cur = i % 2
    prv = 1 - cur

    @pl.when(i == 0)
    def _():
        zb_buf[1] = jnp.zeros(zb_buf.shape[1:], F32)
        cx_buf[1] = jnp.zeros(cx_buf.shape[1:], F32)
        u_buf[1] = jnp.zeros(u_buf.shape[1:], F32)

    sc_b, sc_c, sc_x, cf_a, cf_g = range(5)
    all_cols = slice(None)
    half = ts // 2
    quarter = ts // 4
    chunks_per_half = half // ROW_CHUNK
    u_cur, u_prv = u_buf.at[cur], u_buf.at[prv]
    cx_cur, cx_prv = cx_buf.at[cur], cx_buf.at[prv]

    def proj(c, h):
        rows = slice(h * half, (h + 1) * half)
        z = jnp.dot(h_buf[rows, :], win_ref[:, c * SC_WIDTH:(c + 1) * SC_WIDTH], preferred_element_type=F32)
        if c == sc_b:
            zb_buf[cur, rows, :] = z
        else:
            z_buf[c, rows, :] = z
        return z

    def stage_u(h):
        for r in range(h * half, (h + 1) * half, ROW_CHUNK):
            rows = slice(r, r + ROW_CHUNK)
            _store_lanes(u_cur, CF_HALO + r, z_buf[cf_a, rows, :] * _sigmoid(z_buf[cf_g, rows, :]))

    def stage_cx(h):
        for r in range(h * half, (h + 1) * half, ROW_CHUNK):
            rows = slice(r, r + ROW_CHUNK)
            _store_lanes(cx_cur, SUBLANES + r, z_buf[sc_c, rows, :] * z_buf[sc_x, rows, :])

    def conv(r, after=None):
        rows = slice(r, r + ROW_CHUNK)
        bias = cfb_ref[0:SUBLANES, :]
        if after is not None:
            bias = bias + _schedule_after(after, CF_WIDTH)
        bias = jnp.concatenate([bias] * (ROW_CHUNK // SUBLANES), axis=0)
        acc = _lane_conv(cx_prv, scw_ref, SC_TAPS, SUBLANES + r - (SC_TAPS - 1), ROW_CHUNK)
        cat_buf[rows, 0:SC_WIDTH] = (zb_buf[prv, rows, :] * acc).astype(BF16)
        u = _lane_conv(u_prv, cfw_ref, CF_TAPS, CF_HALO + r - (CF_TAPS - 1), ROW_CHUNK) + bias
        mu = jnp.mean(u, axis=-1, keepdims=True)
        uc = u - mu
        var = jnp.mean(uc * uc, axis=-1, keepdims=True)
        y = (uc * lax.rsqrt(var + EPS) * _bcast_rows(lng_ref, 0, all_cols, ROW_CHUNK)
             + _bcast_rows(lnb_ref, 0, all_cols, ROW_CHUNK))
        cat_buf[rows, SC_WIDTH:SC_WIDTH + CF_WIDTH] = (y * _sigmoid(y)).astype(BF16)

    def out_proj(q):
        rows = slice(q * quarter, (q + 1) * quarter)
        o_ref[0, rows, :] = xp_ref[0, rows, :] + jnp.dot(
            cat_buf[rows, :], wout_ref[...], preferred_element_type=F32)

    keep = (i % nt != 0).astype(F32)
    cx_cur[:, 0:SUBLANES, :] = cx_prv[:, ts:ts + SUBLANES, :] * keep
    u_cur[:, 0:CF_HALO, :] = u_prv[:, ts:ts + CF_HALO, :] * keep
    h_buf[...] = _rms_bf16(x_ref[0], g_ref[...])

    after_chunk = {
        0: [lambda: proj(cf_a, 0)], 1: [lambda: proj(cf_g, 0)],
        2: [lambda: proj(cf_a, 1), lambda: stage_u(0)], 3: [lambda: proj(cf_g, 1)],
        4: [lambda: out_proj(0), lambda: proj(sc_c, 0), lambda: stage_u(1)], 5: [lambda: proj(sc_x, 0)],
        6: [lambda: proj(sc_b, 0), lambda: stage_cx(0)], 7: [lambda: proj(sc_c, 1)],
        8: [lambda: out_proj(1), lambda: proj(sc_x, 1)], 9: [lambda: proj(sc_b, 1)],
        10: [lambda: stage_cx(1)], 12: [lambda: out_proj(2)],
    }
    n_chunks = ts // ROW_CHUNK
    assert n_chunks == 2 * chunks_per_half == 16
    projected = {}
    for k in range(n_chunks):
        conv(k * ROW_CHUNK, projected.get(k))
        for piece in after_chunk.get(k, ()):
            z = piece()
            if z is not None:
                projected[k + 2] = z
    out_proj(3)


def _even_mixer(x, g, w_in, sc_w, cf_w, cf_b, ln_g, ln_b, w_out, ts=SEQ_TILE):
    B, S, D = x.shape
    ev_in = w_in.shape[1]
    nt = S // ts
    n_tiles = B * nt

    def tile(t):
        return (t // nt, t % nt, 0)

    return pl.pallas_call(
        functools.partial(_even_kernel, ts=ts, nt=nt),
        grid=(n_tiles + 1,),
        in_specs=[
            pl.BlockSpec((1, ts, D), lambda i: tile(jnp.minimum(i, n_tiles - 1))),
            pl.BlockSpec((1, ts, D), lambda i: tile(jnp.maximum(i - 1, 0))),
            _const_spec((1, D)),
            _const_spec((D, ev_in)),
            _const_spec(sc_w.shape),
            _const_spec(cf_w.shape),
            _const_spec(cf_b.shape),
            _const_spec(ln_g.shape),
            _const_spec(ln_b.shape),
            _const_spec(w_out.shape),
        ],
        out_specs=pl.BlockSpec((1, ts, D), lambda i: tile(jnp.maximum(i - 1, 0))),
        out_shape=jax.ShapeDtypeStruct((B, S, D), F32),
        scratch_shapes=[
            pltpu.VMEM((ts, D), BF16),
            pltpu.VMEM((ev_in // SC_WIDTH, ts, SC_WIDTH), F32),
            pltpu.VMEM((2, ts, SC_WIDTH), F32),
            pltpu.VMEM((2, SC_WIDTH // LANES, ts + SUBLANES, LANES), F32),
            pltpu.VMEM((2, CF_WIDTH // LANES, ts + CF_HALO, LANES), F32),
            pltpu.VMEM((ts, SC_WIDTH + CF_WIDTH), BF16),
        ],
        compiler_params=pltpu.CompilerParams(
            dimension_semantics=("arbitrary",), vmem_limit_bytes=VMEM_LIMIT),
        name="even_mixer",
    )(x, x, g, w_in, sc_w, cf_w, cf_b, ln_g, ln_b, w_out)


def _ffn_body(x, ng_ref, wg_ref, wu_ref, cw_ref, cb_ref, wd_ref, o_ref, h_buf, g_buf, up_buf, act_buf, ts):
    @pl.when(pl.program_id(1) == 0)
    def _():
        g_buf[:, 0:SUBLANES, :] = jnp.zeros((D_FF // LANES, SUBLANES, LANES), F32)

    def g_rows(c0, start):
        return jnp.concatenate([g_buf[(c0 + l) // LANES, start:start + ROW_CHUNK, :]
                                for l in range(0, FFN_COL_CHUNK, LANES)], axis=1)

    h_buf[...] = _rms_bf16(x, ng_ref[...])
    for c0 in range(0, D_FF, FFN_COL_CHUNK):
        cols = slice(c0, c0 + FFN_COL_CHUNK)
        g = jnp.dot(h_buf[...], wg_ref[:, cols], preferred_element_type=F32)
        for l in range(0, FFN_COL_CHUNK, LANES):
            g_buf[(c0 + l) // LANES, SUBLANES:SUBLANES + ts, :] = g[:, l:l + LANES]
        up_buf[:, cols] = jnp.dot(h_buf[...], wu_ref[:, cols], preferred_element_type=F32)
        w0, w1, w2 = (_bcast_rows(cw_ref, k, cols, ROW_CHUNK) for k in range(3))
        bias = _bcast_rows(cb_ref, 0, cols, ROW_CHUNK)
        for r in range(0, ts, ROW_CHUNK):
            gc = (w0 * g_rows(c0, r + SUBLANES - 2) + w1 * g_rows(c0, r + SUBLANES - 1)
                  + w2 * g_rows(c0, r + SUBLANES) + bias)
            act_buf[r:r + ROW_CHUNK, cols] = (gc * _sigmoid(gc) * up_buf[r:r + ROW_CHUNK, cols]).astype(BF16)
    o_ref[0] = x + jnp.dot(act_buf[...], wd_ref[...], preferred_element_type=F32)
    g_buf[:, 0:SUBLANES, :] = g_buf[:, ts:ts + SUBLANES, :]


def _ffn_kernel(x_ref, ng_ref, wg_ref, wu_ref, cw_ref, cb_ref, wd_ref, o_ref,
                *scratch, ts):
    _ffn_body(x_ref[0], ng_ref, wg_ref, wu_ref, cw_ref, cb_ref, wd_ref, o_ref, *scratch, ts)


def _proj_ffn_kernel(x_ref, ya_ref, yb_ref, wo_ref, ng_ref, wg_ref, wu_ref, cw_ref, cb_ref, wd_ref, o_ref,
                     *scratch, ts):
    wa = ya_ref.shape[-1]
    yb = jnp.concatenate([yb_ref[0, c] for c in range(yb_ref.shape[1])], axis=1)
    x = x_ref[0] + jnp.dot(ya_ref[0].astype(BF16), wo_ref[0:wa, :], preferred_element_type=F32)
    x = x + jnp.dot(yb.astype(BF16), wo_ref[wa:, :], preferred_element_type=F32)
    _ffn_body(x, ng_ref, wg_ref, wu_ref, cw_ref, cb_ref, wd_ref, o_ref, *scratch, ts)


def _conv_ffn(x, ng, wg, wu, cw, cb, wd, mix=None, ts=SEQ_TILE):
    B, S, D = x.shape
    ffn_specs = [
        _const_spec((1, D)),
        _const_spec(wg.shape),
        _const_spec(wu.shape),
        _const_spec(cw.shape),
        _const_spec(cb.shape),
        _const_spec(wd.shape),
    ]
    if mix is None:
        body, ins, specs = _ffn_kernel, (x,), [_tile_spec(D, ts)]
    else:
        ya, yb, wo = mix
        body, ins = _proj_ffn_kernel, (x, ya, yb, wo)
        specs = [_tile_spec(D, ts), _tile_spec(ya.shape[-1], ts), _lane_split_spec(yb.shape[1] * LANES, ts),
                 _const_spec(wo.shape)]
    return pl.pallas_call(
        functools.partial(body, ts=ts),
        grid=(B, S // ts),
        in_specs=specs + ffn_specs,
        out_specs=_tile_spec(D, ts),
        out_shape=jax.ShapeDtypeStruct((B, S, D), F32),
        scratch_shapes=[
            pltpu.VMEM((ts, D), BF16),
            pltpu.VMEM((D_FF // LANES, ts + SUBLANES, LANES), F32),
            pltpu.VMEM((ts, D_FF), F32),
            pltpu.VMEM((ts, D_FF), BF16),
        ],
        compiler_params=pltpu.CompilerParams(
            dimension_semantics=("arbitrary", "arbitrary"), vmem_limit_bytes=VMEM_LIMIT),
        name="conv_ffn" if mix is None else "proj_conv_ffn",
    )(*ins, ng, wg, wu, cw, cb, wd)


def _odd_in_kernel(x_ref, g_ref, win_ref, cw_ref, cb_ref, wa_ref, ba_ref, wx_ref, bx_ref, lam_ref,
                   qg_ref, kg_ref, seg_ref,
                   ylru_ref, qkv0_ref, qkv1_ref, qkv2_ref,
                   h_buf, xin_buf, xc_buf, xcb_buf, ga_buf, gx_buf, gate_buf, t_buf, nq_buf, a_buf, b_buf,
                   al_buf, bl_buf, hin_buf, h_carry, *, ts):
    qkv_refs = (qkv0_ref, qkv1_ref, qkv2_ref)
    @pl.when(pl.program_id(1) == 0)
    def _():
        xin_buf[:, 0:SUBLANES, :] = jnp.zeros((LRU_WIDTH // LANES, SUBLANES, LANES), F32)
        h_carry[...] = jnp.zeros((SUBLANES, LRU_WIDTH), F32)

    all_cols = slice(None)
    h_buf[...] = _rms_bf16(x_ref[0], g_ref[...])

    def proj(c0, width):
        return jnp.dot(h_buf[...], win_ref[:, c0:c0 + width], preferred_element_type=F32)

    _store_lanes(xin_buf, SUBLANES, proj(0, LRU_WIDTH))
    for r in range(0, ts, ROW_CHUNK):
        acc = _bcast_rows(cb_ref, 0, all_cols, ROW_CHUNK)
        for k in range(LRU_TAPS):
            acc += (_bcast_rows(cw_ref, k, all_cols, ROW_CHUNK)
                    * _load_lanes(xin_buf, SUBLANES + r - (LRU_TAPS - 1) + k, ROW_CHUNK))
        xc_buf[r:r + ROW_CHUNK, :] = acc
        xcb_buf[r:r + ROW_CHUNK, :] = acc.astype(BF16)
    xin_buf[:, 0:SUBLANES, :] = xin_buf[:, ts:ts + SUBLANES, :]

    for w_ref, buf in ((wa_ref, ga_buf), (wx_ref, gx_buf)):
        for c0 in range(0, LRU_WIDTH, GATE_BLOCK):
            cols = slice(c0, c0 + GATE_BLOCK)
            buf[:, cols] = jnp.dot(xcb_buf[:, cols], w_ref[cols, cols], preferred_element_type=F32)
    gate_buf[...] = proj(LRU_WIDTH, LRU_WIDTH)

    nlam = -lam_ref[...]
    softplus = jnp.maximum(nlam, 0.0) + jnp.log(1.0 + jnp.exp(-jnp.abs(nlam)))
    decay = jnp.broadcast_to(-LRU_C * softplus, (SUBLANES, LRU_WIDTH))
    row = lax.broadcasted_iota(jnp.int32, (SUBLANES, LRU_WIDTH), 0)

    def scan_group(g, after=None):
        rows = slice(g * SUBLANES, (g + 1) * SUBLANES)
        ba = ba_ref[...]
        if after is not None:
            ba = ba + _schedule_after(after, LRU_WIDTH)
        rg = _sigmoid(ga_buf[rows, :] + ba)
        ig = _sigmoid(gx_buf[rows, :] + bx_ref[...])
        a = jnp.exp(decay * rg)
        b = jnp.sqrt(1.0 - a * a) * (ig * xc_buf[rows, :])
        for s in (1, 2, 4):
            a_prev = jnp.where(row >= s, pltpu.roll(a, s, 0), 1.0)
            b_prev = jnp.where(row >= s, pltpu.roll(b, s, 0), 0.0)
            b = a * b_prev + b
            a = a * a_prev
        a_buf[rows, :] = a
        b_buf[rows, :] = b
        al_buf[rows, :] = jnp.broadcast_to(a[SUBLANES - 1:SUBLANES, :], (SUBLANES, LRU_WIDTH))
        bl_buf[rows, :] = jnp.broadcast_to(b[SUBLANES - 1:SUBLANES, :], (SUBLANES, LRU_WIDTH))

    half = ts // 2
    q0 = 2 * LRU_WIDTH
    qk_rows = 4 * ROW_CHUNK
    norm_args = ((qg_ref, DIL_HEAD_DIM ** -0.5 * math.log2(math.e)), (kg_ref, 1.0))
    v_index = len(norm_args)
    n_split = DIL_OUT // LANES

    def qkv_proj(which, h):
        rows = slice(h * half, (h + 1) * half)
        c0 = q0 + which * DIL_QKV
        z = jnp.dot(h_buf[rows, :], win_ref[:, c0:c0 + DIL_QKV], preferred_element_type=F32)
        if which == v_index:
            for c in range(DIL_QKV // LANES):
                nq_buf[which, c, rows, :] = z[:, c * LANES:(c + 1) * LANES]
        else:
            t_buf[which, rows, :] = z
        return z

    def qk_norm(which, h):
        gain_ref, scale = norm_args[which]
        for r in range(h * half, (h + 1) * half, qk_rows):
            for g0 in range(0, DIL_QKV, DIL_OUT):
                t = t_buf[which, r:r + qk_rows, g0:g0 + DIL_OUT]
                ss = jnp.dot((t * t).astype(BF16), seg_ref[...], preferred_element_type=F32)
                gain = _bcast_rows(gain_ref, 0, slice(g0, g0 + DIL_OUT), qk_rows) * scale
                t = t * lax.rsqrt(ss * (1.0 / DIL_HEAD_DIM) + EPS) * gain
                for l in range(0, DIL_OUT, LANES):
                    nq_buf[which, (g0 + l) // LANES, r:r + qk_rows, :] = t[:, l:l + LANES]
        return t

    def scatter(which, h):
        for g, (_, d) in enumerate(DIL_PATTERNS):
            n = half // d
            for c in range(n_split):
                for r in range(d):
                    rows = slice(h * half, (h + 1) * half) if d == 1 else pl.ds(h * half + r, n, stride=d)
                    qkv_refs[g][0, which, c, r, h * n:(h + 1) * n, :] = (
                        nq_buf[which, g * n_split + c, rows, :].astype(BF16))

    proj_before = {8 * j: p for j, p in enumerate(((0, 0), (0, 1), (1, 0), (1, 1), (2, 0), (2, 1)))}
    norm_after = {14: (0, 0), 22: (0, 1), 30: (1, 0), 38: (1, 1)}
    scatter_after = {18: (0, 0), 26: (0, 1), 34: (1, 0), 42: (1, 1), 46: (v_index, 0), 54: (v_index, 1)}
    waits, normed = {}, []
    for g in range(ts // SUBLANES):
        if g in proj_before:
            waits[g + 6] = qkv_proj(*proj_before[g])
        scan_group(g, waits.get(g))
        if g in norm_after:
            normed.append(qk_norm(*norm_after[g]))
        if g in scatter_after:
            scatter(*scatter_after[g])

    hc = h_carry[...]
    for r in range(0, ts, SUBLANES):
        rows = slice(r, r + SUBLANES)
        hin_buf[rows, :] = hc
        hc = al_buf[rows, :] * hc + bl_buf[rows, :]
    h_carry[...] = hc
    for j, r in enumerate(range(0, ts, ROW_CHUNK)):
        rows = slice(r, r + ROW_CHUNK)
        hs = a_buf[rows, :] * hin_buf[rows, :] + b_buf[rows, :]
        gate = gate_buf[rows, :]
        if j % 4 == 1:
            zero = _schedule_after(normed[j // 4], LRU_WIDTH)
            gate = gate + jnp.concatenate([zero] * (ROW_CHUNK // SUBLANES), axis=0)
        gelu = 0.5 * gate * (1.0 + jnp.tanh(math.sqrt(2.0 / math.pi) * (gate + 0.044715 * (gate * gate * gate))))
        ylru_ref[0, rows, :] = hs * gelu


def _odd_in(x, g, w_in, cw, cb, wa, ba, wx, bx, lam, qg, kg, seg, ts=SEQ_TILE):
    B, S, D = x.shape
    n_split = DIL_OUT // LANES
    return pl.pallas_call(
        functools.partial(_odd_in_kernel, ts=ts),
        grid=(B, S // ts),
        in_specs=[
            _tile_spec(D, ts),
            _const_spec((1, D)),
            _const_spec(w_in.shape),
            _const_spec(cw.shape),
            _const_spec(cb.shape),
            _const_spec(wa.shape),
            _const_spec(ba.shape),
            _const_spec(wx.shape),
            _const_spec(bx.shape),
            _const_spec((1, LRU_WIDTH)),
            _const_spec(qg.shape),
            _const_spec(kg.shape),
            _const_spec(seg.shape),
        ],
        out_specs=[_tile_spec(LRU_WIDTH, ts)] + [
            pl.BlockSpec((1, 3, n_split, d, ts // d, LANES), lambda b, j: (b, 0, 0, 0, j, 0))
            for _, d in DIL_PATTERNS],
        out_shape=[jax.ShapeDtypeStruct((B, S, LRU_WIDTH), F32)] + [
            jax.ShapeDtypeStruct((B, 3, n_split, d, S // d, LANES), BF16) for _, d in DIL_PATTERNS],
        scratch_shapes=[
            pltpu.VMEM((ts, D), BF16),
            pltpu.VMEM((LRU_WIDTH // LANES, ts + SUBLANES, LANES), F32),
            pltpu.VMEM((ts, LRU_WIDTH), F32),
            pltpu.VMEM((ts, LRU_WIDTH), BF16),
            pltpu.VMEM((ts, LRU_WIDTH), F32),
            pltpu.VMEM((ts, LRU_WIDTH), F32),
            pltpu.VMEM((ts, LRU_WIDTH), F32),
            pltpu.VMEM((2, ts, DIL_QKV), F32),
            pltpu.VMEM((3, DIL_QKV // LANES, ts, LANES), F32),
            pltpu.VMEM((ts, LRU_WIDTH), F32),
            pltpu.VMEM((ts, LRU_WIDTH), F32),
            pltpu.VMEM((ts, LRU_WIDTH), F32),
            pltpu.VMEM((ts, LRU_WIDTH), F32),
            pltpu.VMEM((ts, LRU_WIDTH), F32),
            pltpu.VMEM((SUBLANES, LRU_WIDTH), F32),
        ],
        compiler_params=pltpu.CompilerParams(
            dimension_semantics=("arbitrary", "arbitrary"), vmem_limit_bytes=VMEM_LIMIT),
        name="odd_in_lru",
    )(x, g, w_in, cw, cb, wa, ba, wx, bx, lam, qg, kg, seg)


def _attn_kernel(*refs, dilation, prev_dilation, last, seq):
    qkv_ref, refs = refs[0], refs[1:]
    if prev_dilation is not None:
        (prev_o_ref, prev_lse_ref), refs = refs[:2], refs[2:]
    if last:
        o_ref, bias_buf = refs
    else:
        o_ref, lse_ref, bias_buf = refs
    blk = ATTN_BLOCK
    nb = seq // dilation // blk
    n_split = DIL_OUT // LANES
    qi = lax.broadcasted_iota(jnp.int32, (DIL_HEADS * blk, 2 * blk), 0) % blk
    kj = lax.broadcasted_iota(jnp.int32, (DIL_HEADS * blk, 2 * blk), 1)
    for t in range(2):
        steps = qi - kj + t * blk
        bias_buf[t] = jnp.where((steps >= 0) & (steps <= blk), 0.0, -jnp.inf).astype(F32)

    lane_head = lax.broadcasted_iota(jnp.int32, (1, DIL_OUT), 1) // DIL_HEAD_DIM
    is_head = [lane_head == h for h in range(DIL_HEADS)]

    def fold(t):
        out = t[(DIL_HEADS - 1) * blk:DIL_HEADS * blk]
        for h in range(DIL_HEADS - 2, -1, -1):
            out = jnp.where(is_head[h], t[h * blk:(h + 1) * blk], out)
        return out

    def load(which, res, rows):
        return jnp.concatenate([qkv_ref[0, which, c, res, rows, :] for c in range(n_split)], axis=1)

    def body(idx, carry):
        res = idx // nb
        b = idx % nb
        kb = jnp.maximum(b - 1, 0)
        q_rows = pl.ds(pl.multiple_of(b * blk, blk), blk)
        k_rows = pl.ds(pl.multiple_of(kb * blk, blk), 2 * blk)
        qb = load(0, res, q_rows)
        zero = jnp.zeros_like(qb)
        qs = jnp.concatenate([jnp.where(is_head[h], qb, zero) for h in range(DIL_HEADS)], axis=0)
        s = lax.dot_general(qs, load(1, res, k_rows), (((1,), (1,)), ((), ())), preferred_element_type=F32)
        s = s + bias_buf[jnp.minimum(b, 1)]
        m = jnp.max(s, axis=-1, keepdims=True)
        p = jnp.exp2(s - m)
        l = jnp.sum(p, axis=-1, keepdims=True)
        o = jnp.dot(p.astype(BF16), load(2, res, k_rows), preferred_element_type=F32)
        o_tok, m_tok, l_tok = fold(o), fold(m), fold(l)
        o_tok = o_tok / l_tok
        lse = m_tok + jnp.log2(l_tok)
        if prev_dilation is not None:
            step = dilation // prev_dilation
            p_rows = pl.ds(b * (blk * step) + res // prev_dilation, blk, stride=step)
            p_res = res % prev_dilation
            prev_o = jnp.concatenate([prev_o_ref[0, c, p_res, p_rows, :] for c in range(n_split)], axis=1)
            prev_lse = jnp.concatenate([prev_lse_ref[0, c, p_res, p_rows, :] for c in range(n_split)], axis=1)
            top = jnp.maximum(prev_lse, lse)
            w_prev = jnp.exp2(prev_lse - top)
            w_new = jnp.exp2(lse - top)
            den = w_prev + w_new
            o_tok = (w_prev * prev_o + w_new * o_tok) / den
            lse = top + jnp.log2(den)
        if last:
            seq_rows = pl.ds(b * (blk * dilation) + res, blk, stride=dilation)
            for c in range(n_split):
                o_ref[0, c, seq_rows, :] = o_tok[:, c * LANES:(c + 1) * LANES]
        else:
            for c in range(n_split):
                o_ref[0, c, res, q_rows, :] = o_tok[:, c * LANES:(c + 1) * LANES]
                lse_ref[0, c, res, q_rows, :] = lse[:, c * LANES:(c + 1) * LANES]
        return carry

    lax.fori_loop(0, dilation * nb, body, 0, unroll=4)


def _dilated_attention(qkv_groups):
    def batch_block(shape):
        return pl.BlockSpec((1,) + shape[1:], lambda b: (b,) + (0,) * (len(shape) - 1))

    merged, prev_d = (), None
    for g, qkv in enumerate(qkv_groups):
        B, _, n_split, d, L, _ = qkv.shape
        assert DIL_PATTERNS[g] == (ATTN_BLOCK * d, d)
        last = g == len(qkv_groups) - 1
        if last:
            out_shape = [jax.ShapeDtypeStruct((B, n_split, d * L, LANES), F32)]
        else:
            out_shape = [jax.ShapeDtypeStruct((B, n_split, d, L, LANES), F32)] * 2
        ins = (qkv,) + tuple(merged)
        merged = pl.pallas_call(
            functools.partial(_attn_kernel, dilation=d, prev_dilation=prev_d, last=last, seq=d * L),
            grid=(B,),
            in_specs=[batch_block(a.shape) for a in ins],
            out_specs=[batch_block(s.shape) for s in out_shape],
            out_shape=out_shape,
            scratch_shapes=[pltpu.VMEM((2, DIL_HEADS * ATTN_BLOCK, 2 * ATTN_BLOCK), F32)],
            compiler_params=pltpu.CompilerParams(
                dimension_semantics=("arbitrary",), vmem_limit_bytes=VMEM_LIMIT),
            name=f"dilated_attention_d{d}",
        )(*ins)
        prev_d = d
    return merged[0]


def _block_diag(w):
    H, n, _ = w.shape
    eye = jnp.eye(H, dtype=w.dtype)
    return (eye[:, None, :, None] * w[:, :, None, :]).reshape(H * n, H * n)


def kernel(x, mix_norm_g, ffn_norm_g, ev_w_in, ev_sc_conv_w, ev_cf_conv_w, ev_cf_conv_b, ev_cf_ln_g, ev_cf_ln_b, ev_w_out, od_w_in, od_lru_conv_w, od_lru_conv_b, od_lru_wa, od_lru_ba, od_lru_wx, od_lru_bx, od_lru_lam, od_q_norm_g, od_k_norm_g, od_w_out, ffn_w_gate, ffn_w_up, ffn_conv_w, ffn_conv_b, ffn_w_down):
    row = lambda p: p.reshape(1, -1)
    bf = lambda w: w.astype(BF16)
    n_heads = DIL_QKV // DIL_HEAD_DIM
    head_of = jnp.arange(DIL_OUT) // DIL_HEAD_DIM
    seg = (head_of[:, None] == head_of[None, :]).astype(BF16)

    def ffn(x, layer, mix=None):
        return _conv_ffn(x, row(ffn_norm_g[layer]), bf(ffn_w_gate[layer]), bf(ffn_w_up[layer]),
                         _rep8(ffn_conv_w[layer]), _rep8(ffn_conv_b[layer]), bf(ffn_w_down[layer]), mix=mix)

    x = _even_mixer(x, row(mix_norm_g[0]), bf(ev_w_in[0]), _rep8(ev_sc_conv_w[0]), _rep8(ev_cf_conv_w[0]),
                    _rep8(ev_cf_conv_b[0]), _rep8(ev_cf_ln_g[0]), _rep8(ev_cf_ln_b[0]), bf(ev_w_out[0]))
    x = ffn(x, 0)
    y_lru, *qkv_groups = _odd_in(
        x, row(mix_norm_g[1]), bf(od_w_in[0]), _rep8(od_lru_conv_w[0]), _rep8(od_lru_conv_b[0]),
        bf(_block_diag(od_lru_wa[0])), _rep8(od_lru_ba[0]), bf(_block_diag(od_lru_wx[0])), _rep8(od_lru_bx[0]),
        row(od_lru_lam[0]), _rep8(jnp.tile(od_q_norm_g[0], n_heads)), _rep8(jnp.tile(od_k_norm_g[0], n_heads)), seg)
    y_att = _dilated_attention(qkv_groups)
    return ffn(x, 1, mix=(y_lru, y_att, bf(od_w_out[0])))
```

```python
import functools
import math

import jax
import jax.numpy as jnp
from jax import lax
from jax.experimental import pallas as pl
from jax.experimental.pallas import tpu as pltpu

F32 = jnp.float32
BF16 = jnp.bfloat16

D_MODEL = 1024
SC_WIDTH = 512
SC_TAPS = 3
CF_WIDTH = 512
CF_TAPS = 31
LRU_WIDTH = 512
LRU_TAPS = 4
LRU_C = 8.0
MXU_WIDTH = 256
GATE_BLOCK = MXU_WIDTH
DIL_PATTERNS = ((128, 1), (512, 4), (2048, 16))
DIL_HEADS = 4
DIL_HEAD_DIM = 64
DIL_OUT = DIL_HEADS * DIL_HEAD_DIM
DIL_QKV = len(DIL_PATTERNS) * DIL_OUT
ATTN_BLOCK = 128
D_FF = 2816
EPS = 1e-6

SUBLANES = 8
LANES = 128
SEQ_TILE = 512
CF_HALO = 32
ROW_CHUNK = 32
FFN_COL_CHUNK = 256
VMEM_LIMIT = 56 * 1024 * 1024


def _rms_bf16(x, g):
    y = x * lax.rsqrt(jnp.mean(x * x, axis=-1, keepdims=True) + EPS)
    return (y * g).astype(BF16)


def _sigmoid(x):
    return 1.0 / (1.0 + jnp.exp(-x))


def _rep8(p):
    return jnp.repeat(p.reshape(-1, p.shape[-1]), SUBLANES, axis=0)


def _bcast_rows(ref, k, cols, n):
    w = ref[k * SUBLANES:(k + 1) * SUBLANES, cols]
    return jnp.concatenate([w] * (n // SUBLANES), axis=0)


def _schedule_after(val, width):
    bits = lax.bitcast_convert_type(val[-SUBLANES:, -LANES:].astype(F32), jnp.int32)
    zero = lax.shift_right_logical(lax.shift_right_logical(bits, 16), 16).astype(F32)
    return jnp.concatenate([zero] * (width // LANES), axis=1)


def _load_lanes(ref, start, n):
    return jnp.concatenate([ref[c, start:start + n, :] for c in range(ref.shape[0])], axis=1)


def _lane_conv(ref, w_ref, n_taps, start, n):
    out = []
    for c in range(ref.shape[0]):
        cols = slice(c * LANES, (c + 1) * LANES)
        acc = None
        for k in range(n_taps):
            term = _bcast_rows(w_ref, k, cols, n) * ref[c, start + k:start + k + n, :]
            acc = term if acc is None else acc + term
        out.append(acc)
    return jnp.concatenate(out, axis=1)


def _store_lanes(ref, start, val):
    for c in range(ref.shape[0]):
        ref[c, start:start + val.shape[0], :] = val[:, c * LANES:(c + 1) * LANES]


def _const_spec(shape):
    return pl.BlockSpec(shape, lambda *_: (0,) * len(shape), pipeline_mode=pl.Buffered(1))


def _tile_spec(width, ts):
    return pl.BlockSpec((1, ts, width), lambda b, j: (b, j, 0))


def _lane_split_spec(width, ts):
    return pl.BlockSpec((1, width // LANES, ts, LANES), lambda b, j: (b, 0, j, 0))


def _even_kernel(x_ref, g_ref, win_ref, scw_ref, cfw_ref, cfb_ref, lng_ref, lnb_ref, wout_ref,
                 o_ref, h_buf, z_buf, cx_buf, u_buf, cat_buf, *, ts):
    @pl.when(pl.program_id(1) == 0)
    def _():
        cx_buf[:, 0:SUBLANES, :] = jnp.zeros((SC_WIDTH // LANES, SUBLANES, LANES), F32)
        u_buf[:, 0:CF_HALO, :] = jnp.zeros((CF_WIDTH // LANES, CF_HALO, LANES), F32)

    sc_b, sc_c, sc_x, cf_a, cf_g = range(5)
    all_cols = slice(None)
    half, quarter = ts // 2, ts // 4
    n_chunks = ts // ROW_CHUNK
    per_quarter = n_chunks // 4
    h_buf[...] = _rms_bf16(x_ref[0], g_ref[...])

    def proj_rows(c, h):
        rows = slice(h * half, (h + 1) * half)
        z_buf[c, rows, :] = jnp.dot(h_buf[rows, :], win_ref[:, c * SC_WIDTH:(c + 1) * SC_WIDTH],
                                    preferred_element_type=F32)

    def proj_cols(c, h):
        cols = slice(h * MXU_WIDTH, (h + 1) * MXU_WIDTH)
        z_buf[c, :, cols] = jnp.dot(h_buf[...], win_ref[:, c * SC_WIDTH + cols.start:c * SC_WIDTH + cols.stop],
                                    preferred_element_type=F32)

    def stage_u(k):
        rows = slice(k * ROW_CHUNK, (k + 1) * ROW_CHUNK)
        _store_lanes(u_buf, CF_HALO + rows.start, z_buf[cf_a, rows, :] * _sigmoid(z_buf[cf_g, rows, :]))

    def stage_cx(k):
        rows = slice(k * ROW_CHUNK, (k + 1) * ROW_CHUNK)
        _store_lanes(cx_buf, SUBLANES + rows.start, z_buf[sc_c, rows, :] * z_buf[sc_x, rows, :])

    def conv_sc(k):
        r = k * ROW_CHUNK
        acc = _lane_conv(cx_buf, scw_ref, SC_TAPS, SUBLANES + r - (SC_TAPS - 1), ROW_CHUNK)
        cat_buf[r:r + ROW_CHUNK, 0:SC_WIDTH] = (z_buf[sc_b, r:r + ROW_CHUNK, :] * acc).astype(BF16)

    def conv_cf(k):
        r = k * ROW_CHUNK
        u = _lane_conv(u_buf, cfw_ref, CF_TAPS, CF_HALO + r - (CF_TAPS - 1), ROW_CHUNK)
        u += _bcast_rows(cfb_ref, 0, all_cols, ROW_CHUNK)
        mu = jnp.mean(u, axis=-1, keepdims=True)
        uc = u - mu
        var = jnp.mean(uc * uc, axis=-1, keepdims=True)
        y = (uc * lax.rsqrt(var + EPS) * _bcast_rows(lng_ref, 0, all_cols, ROW_CHUNK)
             + _bcast_rows(lnb_ref, 0, all_cols, ROW_CHUNK))
        cat_buf[r:r + ROW_CHUNK, SC_WIDTH:SC_WIDTH + CF_WIDTH] = (y * _sigmoid(y)).astype(BF16)

    def out_proj(q):
        rows = slice(q * quarter, (q + 1) * quarter)
        for k in range(q * per_quarter, (q + 1) * per_quarter):
            conv_sc(k)
        o_ref[0, rows, :] = x_ref[0, rows, :] + jnp.dot(
            cat_buf[rows, :], wout_ref[...], preferred_element_type=F32)

    proj_rows(cf_a, 0)
    proj_rows(cf_g, 0)
    for k in range(n_chunks // 2):
        stage_u(k)
    after_chunk = {
        0: [lambda: proj_rows(cf_a, 1)],
        1: [lambda: proj_rows(cf_g, 1)] + [functools.partial(stage_u, k) for k in range(n_chunks // 2, n_chunks)],
        2: [lambda: proj_cols(sc_c, 0)], 3: [lambda: proj_cols(sc_c, 1)],
        4: [lambda: proj_cols(sc_x, 0)], 5: [lambda: proj_cols(sc_x, 1)],
        6: [lambda: proj_cols(sc_b, 0)], 7: [lambda: proj_cols(sc_b, 1)],
        8: [functools.partial(stage_cx, k) for k in range(n_chunks)],
        9: [lambda: out_proj(0)], 10: [lambda: out_proj(1)], 12: [lambda: out_proj(2)], 15: [lambda: out_proj(3)],
    }
    assert n_chunks == 16
    for k in range(n_chunks):
        conv_cf(k)
        for piece in after_chunk.get(k, ()):
            piece()

    cx_buf[:, 0:SUBLANES, :] = cx_buf[:, ts:ts + SUBLANES, :]
    u_buf[:, 0:CF_HALO, :] = u_buf[:, ts:ts + CF_HALO, :]


def _even_mixer(x, g, w_in, sc_w, cf_w, cf_b, ln_g, ln_b, w_out, ts=SEQ_TILE):
    B, S, D = x.shape
    ev_in = w_in.shape[1]
    return pl.pallas_call(
        functools.partial(_even_kernel, ts=ts),
        grid=(B, S // ts),
        in_specs=[
            _tile_spec(D, ts),
            _const_spec((1, D)),
            _const_spec((D, ev_in)),
            _const_spec(sc_w.shape),
            _const_spec(cf_w.shape),
            _const_spec(cf_b.shape),
            _const_spec(ln_g.shape),
            _const_spec(ln_b.shape),
            _const_spec(w_out.shape),
        ],
        out_specs=_tile_spec(D, ts),
        out_shape=jax.ShapeDtypeStruct((B, S, D), F32),
        scratch_shapes=[
            pltpu.VMEM((ts, D), BF16),
            pltpu.VMEM((ev_in // SC_WIDTH, ts, SC_WIDTH), F32),
            pltpu.VMEM((SC_WIDTH // LANES, ts + SUBLANES, LANES), F32),
            pltpu.VMEM((CF_WIDTH // LANES, ts + CF_HALO, LANES), F32),
            pltpu.VMEM((ts, SC_WIDTH + CF_WIDTH), BF16),
        ],
        compiler_params=pltpu.CompilerParams(
            dimension_semantics=("arbitrary", "arbitrary"), vmem_limit_bytes=VMEM_LIMIT),
        name="even_mixer",
    )(x, g, w_in, sc_w, cf_w, cf_b, ln_g, ln_b, w_out)


def _ffn_body(x, ng_ref, wg_ref, wu_ref, cw_ref, cb_ref, wd_ref, o_ref, h_buf, g_buf, up_buf, act_buf, ts):
    @pl.when(pl.program_id(1) == 0)
    def _():
        g_buf[:, 0:SUBLANES, :] = jnp.zeros((D_FF // LANES, SUBLANES, LANES), F32)

    def g_rows(c0, start):
        return jnp.concatenate([g_buf[(c0 + l) // LANES, start:start + ROW_CHUNK, :]
                                for l in range(0, FFN_COL_CHUNK, LANES)], axis=1)

    h_buf[...] = _rms_bf16(x, ng_ref[...])
    for c0 in range(0, D_FF, FFN_COL_CHUNK):
        cols = slice(c0, c0 + FFN_COL_CHUNK)
        g = jnp.dot(h_buf[...], wg_ref[:, cols], preferred_element_type=F32)
        for l in range(0, FFN_COL_CHUNK, LANES):
            g_buf[(c0 + l) // LANES, SUBLANES:SUBLANES + ts, :] = g[:, l:l + LANES]
        up_buf[:, cols] = jnp.dot(h_buf[...], wu_ref[:, cols], preferred_element_type=F32)
        w0, w1, w2 = (_bcast_rows(cw_ref, k, cols, ROW_CHUNK) for k in range(3))
        bias = _bcast_rows(cb_ref, 0, cols, ROW_CHUNK)
        for r in range(0, ts, ROW_CHUNK):
            gc = (w0 * g_rows(c0, r + SUBLANES - 2) + w1 * g_rows(c0, r + SUBLANES - 1)
                  + w2 * g_rows(c0, r + SUBLANES) + bias)
            act_buf[r:r + ROW_CHUNK, cols] = (gc * _sigmoid(gc) * up_buf[r:r + ROW_CHUNK, cols]).astype(BF16)
    o_ref[0] = x + jnp.dot(act_buf[...], wd_ref[...], preferred_element_type=F32)
    g_buf[:, 0:SUBLANES, :] = g_buf[:, ts:ts + SUBLANES, :]


def _ffn_kernel(x_ref, ng_ref, wg_ref, wu_ref, cw_ref, cb_ref, wd_ref, o_ref,
                *scratch, ts):
    _ffn_body(x_ref[0], ng_ref, wg_ref, wu_ref, cw_ref, cb_ref, wd_ref, o_ref, *scratch, ts)


def _proj_ffn_kernel(x_ref, ya_ref, yb_ref, wo_ref, ng_ref, wg_ref, wu_ref, cw_ref, cb_ref, wd_ref, o_ref,
                     *scratch, ts):
    wa = ya_ref.shape[-1]
    yb = jnp.concatenate([yb_ref[0, c] for c in range(yb_ref.shape[1])], axis=1)
    x = x_ref[0] + jnp.dot(ya_ref[0].astype(BF16), wo_ref[0:wa, :], preferred_element_type=F32)
    x = x + jnp.dot(yb.astype(BF16), wo_ref[wa:, :], preferred_element_type=F32)
    _ffn_body(x, ng_ref, wg_ref, wu_ref, cw_ref, cb_ref, wd_ref, o_ref, *scratch, ts)


def _layer_spec(stacked, layer):
    zeros = (0,) * (stacked.ndim - 1)
    return pl.BlockSpec((None,) + stacked.shape[1:], lambda *_: (layer,) + zeros, pipeline_mode=pl.Buffered(1))


def _conv_ffn(x, layer, ng, wg, wu, cw, cb, wd, mix=None, ts=SEQ_TILE):
    B, S, D = x.shape
    ffn_specs = [
        _const_spec((1, D)),
        _layer_spec(wg, layer),
        _layer_spec(wu, layer),
        _const_spec(cw.shape),
        _const_spec(cb.shape),
        _layer_spec(wd, layer),
    ]
    if mix is None:
        body, ins, specs = _ffn_kernel, (x,), [_tile_spec(D, ts)]
    else:
        ya, yb, wo = mix
        body, ins = _proj_ffn_kernel, (x, ya, yb, wo)
        specs = [_tile_spec(D, ts), _tile_spec(ya.shape[-1], ts), _lane_split_spec(yb.shape[1] * LANES, ts),
                 _const_spec(wo.shape)]
    return pl.pallas_call(
        functools.partial(body, ts=ts),
        grid=(B, S // ts),
        in_specs=specs + ffn_specs,
        out_specs=_tile_spec(D, ts),
        out_shape=jax.ShapeDtypeStruct((B, S, D), F32),
        scratch_shapes=[
            pltpu.VMEM((ts, D), BF16),
            pltpu.VMEM((D_FF // LANES, ts + SUBLANES, LANES), F32),
            pltpu.VMEM((ts, D_FF), F32),
            pltpu.VMEM((ts, D_FF), BF16),
        ],
        compiler_params=pltpu.CompilerParams(
            dimension_semantics=("arbitrary", "arbitrary"), vmem_limit_bytes=VMEM_LIMIT),
        name="conv_ffn" if mix is None else "proj_conv_ffn",
    )(*ins, ng, wg, wu, cw, cb, wd)


def _odd_in_kernel(x_ref, g_ref, win_ref, cw_ref, cb_ref, wa_ref, ba_ref, wx_ref, bx_ref, lam_ref,
                   qg_ref, kg_ref, seg_ref,
                   ylru_ref, qkv0_ref, qkv1_ref, qkv2_ref,
                   h_buf, xin_buf, xc_buf, xcb_buf, ga_buf, gx_buf, gate_buf, t_buf, nq_buf, a_buf, b_buf,
                   al_buf, bl_buf, hin_buf, h_carry, *, ts):
    qkv_refs = (qkv0_ref, qkv1_ref, qkv2_ref)
    @pl.when(pl.program_id(1) == 0)
    def _():
        xin_buf[:, 0:SUBLANES, :] = jnp.zeros((LRU_WIDTH // LANES, SUBLANES, LANES), F32)
        h_carry[...] = jnp.zeros((SUBLANES, LRU_WIDTH), F32)

    all_cols = slice(None)
    h_buf[...] = _rms_bf16(x_ref[0], g_ref[...])

    def proj(c0, width):
        return jnp.dot(h_buf[...], win_ref[:, c0:c0 + width], preferred_element_type=F32)

    _store_lanes(xin_buf, SUBLANES, proj(0, LRU_WIDTH))
    for r in range(0, ts, ROW_CHUNK):
        acc = _bcast_rows(cb_ref, 0, all_cols, ROW_CHUNK)
        for k in range(LRU_TAPS):
            acc += (_bcast_rows(cw_ref, k, all_cols, ROW_CHUNK)
                    * _load_lanes(xin_buf, SUBLANES + r - (LRU_TAPS - 1) + k, ROW_CHUNK))
        xc_buf[r:r + ROW_CHUNK, :] = acc
        xcb_buf[r:r + ROW_CHUNK, :] = acc.astype(BF16)
    xin_buf[:, 0:SUBLANES, :] = xin_buf[:, ts:ts + SUBLANES, :]

    for w_ref, buf in ((wa_ref, ga_buf), (wx_ref, gx_buf)):
        for c0 in range(0, LRU_WIDTH, GATE_BLOCK):
            cols = slice(c0, c0 + GATE_BLOCK)
            buf[:, cols] = jnp.dot(xcb_buf[:, cols], w_ref[cols, cols], preferred_element_type=F32)
    gate_buf[...] = proj(LRU_WIDTH, LRU_WIDTH)

    nlam = -lam_ref[...]
    softplus = jnp.maximum(nlam, 0.0) + jnp.log(1.0 + jnp.exp(-jnp.abs(nlam)))
    decay = jnp.broadcast_to(-LRU_C * softplus, (SUBLANES, LRU_WIDTH))
    row = lax.broadcasted_iota(jnp.int32, (SUBLANES, LRU_WIDTH), 0)

    def scan_group(g, after=None):
        rows = slice(g * SUBLANES, (g + 1) * SUBLANES)
        ba = ba_ref[...]
        if after is not None:
            ba = ba + _schedule_after(after, LRU_WIDTH)
        rg = _sigmoid(ga_buf[rows, :] + ba)
        ig = _sigmoid(gx_buf[rows, :] + bx_ref[...])
        a = jnp.exp(decay * rg)
        b = jnp.sqrt(1.0 - a * a) * (ig * xc_buf[rows, :])
        for s in (1, 2, 4):
            a_prev = jnp.where(row >= s, pltpu.roll(a, s, 0), 1.0)
            b_prev = jnp.where(row >= s, pltpu.roll(b, s, 0), 0.0)
            b = a * b_prev + b
            a = a * a_prev
        a_buf[rows, :] = a
        b_buf[rows, :] = b
        al_buf[rows, :] = jnp.broadcast_to(a[SUBLANES - 1:SUBLANES, :], (SUBLANES, LRU_WIDTH))
        bl_buf[rows, :] = jnp.broadcast_to(b[SUBLANES - 1:SUBLANES, :], (SUBLANES, LRU_WIDTH))

    half = ts // 2
    q0 = 2 * LRU_WIDTH
    qk_rows = 4 * ROW_CHUNK
    norm_args = ((qg_ref, DIL_HEAD_DIM ** -0.5 * math.log2(math.e)), (kg_ref, 1.0))
    v_index = len(norm_args)
    n_split = DIL_OUT // LANES

    def qkv_proj(which, h):
        rows = slice(h * half, (h + 1) * half)
        c0 = q0 + which * DIL_QKV
        z = jnp.dot(h_buf[rows, :], win_ref[:, c0:c0 + DIL_QKV], preferred_element_type=F32)
        if which == v_index:
            for c in range(DIL_QKV // LANES):
                nq_buf[which, c, rows, :] = z[:, c * LANES:(c + 1) * LANES]
        else:
            t_buf[which, rows, :] = z
        return z

    def qk_norm(which, h):
        gain_ref, scale = norm_args[which]
        for r in range(h * half, (h + 1) * half, qk_rows):
            for g0 in range(0, DIL_QKV, DIL_OUT):
                t = t_buf[which, r:r + qk_rows, g0:g0 + DIL_OUT]
                ss = jnp.dot((t * t).astype(BF16), seg_ref[...], preferred_element_type=F32)
                gain = _bcast_rows(gain_ref, 0, slice(g0, g0 + DIL_OUT), qk_rows) * scale
                t = t * lax.rsqrt(ss * (1.0 / DIL_HEAD_DIM) + EPS) * gain
                for l in range(0, DIL_OUT, LANES):
                    nq_buf[which, (g0 + l) // LANES, r:r + qk_rows, :] = t[:, l:l + LANES]
        return t

    def scatter(which, h):
        for g, (_, d) in enumerate(DIL_PATTERNS):
            n = half // d
            for c in range(n_split):
                for r in range(d):
                    rows = slice(h * half, (h + 1) * half) if d == 1 else pl.ds(h * half + r, n, stride=d)
                    qkv_refs[g][0, which, c, r, h * n:(h + 1) * n, :] = (
                        nq_buf[which, g * n_split + c, rows, :].astype(BF16))

    proj_before = {8 * j: p for j, p in enumerate(((0, 0), (0, 1), (1, 0), (1, 1), (2, 0), (2, 1)))}
    norm_after = {14: (0, 0), 22: (0, 1), 30: (1, 0), 38: (1, 1)}
    scatter_after = {18: (0, 0), 26: (0, 1), 34: (1, 0), 42: (1, 1), 46: (v_index, 0), 54: (v_index, 1)}
    waits, normed = {}, []
    for g in range(ts // SUBLANES):
        if g in proj_before:
            waits[g + 6] = qkv_proj(*proj_before[g])
        scan_group(g, waits.get(g))
        if g in norm_after:
            normed.append(qk_norm(*norm_after[g]))
        if g in scatter_after:
            scatter(*scatter_after[g])

    hc = h_carry[...]
    for r in range(0, ts, SUBLANES):
        rows = slice(r, r + SUBLANES)
        hin_buf[rows, :] = hc
        hc = al_buf[rows, :] * hc + bl_buf[rows, :]
    h_carry[...] = hc
    for j, r in enumerate(range(0, ts, ROW_CHUNK)):
        rows = slice(r, r + ROW_CHUNK)
        hs = a_buf[rows, :] * hin_buf[rows, :] + b_buf[rows, :]
        gate = gate_buf[rows, :]
        if j % 4 == 1:
            zero = _schedule_after(normed[j // 4], LRU_WIDTH)
            gate = gate + jnp.concatenate([zero] * (ROW_CHUNK // SUBLANES), axis=0)
        gelu = 0.5 * gate * (1.0 + jnp.tanh(math.sqrt(2.0 / math.pi) * (gate + 0.044715 * (gate * gate * gate))))
        ylru_ref[0, rows, :] = hs * gelu


def _odd_in(x, g, w_in, cw, cb, wa, ba, wx, bx, lam, qg, kg, seg, ts=SEQ_TILE):
    B, S, D = x.shape
    n_split = DIL_OUT // LANES
    return pl.pallas_call(
        functools.partial(_odd_in_kernel, ts=ts),
        grid=(B, S // ts),
        in_specs=[
            _tile_spec(D, ts),
            _const_spec((1, D)),
            _const_spec(w_in.shape),
            _const_spec(cw.shape),
            _const_spec(cb.shape),
            _const_spec(wa.shape),
            _const_spec(ba.shape),
            _const_spec(wx.shape),
            _const_spec(bx.shape),
            _const_spec((1, LRU_WIDTH)),
            _const_spec(qg.shape),
            _const_spec(kg.shape),
            _const_spec(seg.shape),
        ],
        out_specs=[_tile_spec(LRU_WIDTH, ts)] + [
            pl.BlockSpec((1, 3, n_split, d, ts // d, LANES), lambda b, j: (b, 0, 0, 0, j, 0))
            for _, d in DIL_PATTERNS],
        out_shape=[jax.ShapeDtypeStruct((B, S, LRU_WIDTH), F32)] + [
            jax.ShapeDtypeStruct((B, 3, n_split, d, S // d, LANES), BF16) for _, d in DIL_PATTERNS],
        scratch_shapes=[
            pltpu.VMEM((ts, D), BF16),
            pltpu.VMEM((LRU_WIDTH // LANES, ts + SUBLANES, LANES), F32),
            pltpu.VMEM((ts, LRU_WIDTH), F32),
            pltpu.VMEM((ts, LRU_WIDTH), BF16),
            pltpu.VMEM((ts, LRU_WIDTH), F32),
            pltpu.VMEM((ts, LRU_WIDTH), F32),
            pltpu.VMEM((ts, LRU_WIDTH), F32),
            pltpu.VMEM((2, ts, DIL_QKV), F32),
            pltpu.VMEM((3, DIL_QKV // LANES, ts, LANES), F32),
            pltpu.VMEM((ts, LRU_WIDTH), F32),
            pltpu.VMEM((ts, LRU_WIDTH), F32),
            pltpu.VMEM((ts, LRU_WIDTH), F32),
            pltpu.VMEM((ts, LRU_WIDTH), F32),
            pltpu.VMEM((ts, LRU_WIDTH), F32),
            pltpu.VMEM((SUBLANES, LRU_WIDTH), F32),
        ],
        compiler_params=pltpu.CompilerParams(
            dimension_semantics=("arbitrary", "arbitrary"), vmem_limit_bytes=VMEM_LIMIT),
        name="odd_in_lru",
    )(x, g, w_in, cw, cb, wa, ba, wx, bx, lam, qg, kg, seg)


def _attn_kernel(*refs, dilation, prev_dilation, last, seq):
    qkv_ref, refs = refs[0], refs[1:]
    if prev_dilation is not None:
        (prev_o_ref, prev_lse_ref), refs = refs[:2], refs[2:]
    if last:
        o_ref, bias_buf = refs
    else:
        o_ref, lse_ref, bias_buf = refs
    blk = ATTN_BLOCK
    nb = seq // dilation // blk
    n_split = DIL_OUT // LANES
    qi = lax.broadcasted_iota(jnp.int32, (DIL_HEADS * blk, 2 * blk), 0) % blk
    kj = lax.broadcasted_iota(jnp.int32, (DIL_HEADS * blk, 2 * blk), 1)
    for t in range(2):
        steps = qi - kj + t * blk
        bias_buf[t] = jnp.where((steps >= 0) & (steps <= blk), 0.0, -jnp.inf).astype(F32)

    lane_head = lax.broadcasted_iota(jnp.int32, (1, DIL_OUT), 1) // DIL_HEAD_DIM
    is_head = [lane_head == h for h in range(DIL_HEADS)]

    def fold(t):
        out = t[(DIL_HEADS - 1) * blk:DIL_HEADS * blk]
        for h in range(DIL_HEADS - 2, -1, -1):
            out = jnp.where(is_head[h], t[h * blk:(h + 1) * blk], out)
        return out

    def load(which, res, rows):
        return jnp.concatenate([qkv_ref[0, which, c, res, rows, :] for c in range(n_split)], axis=1)

    def body(idx, carry):
        res = idx // nb
        b = idx % nb
        kb = jnp.maximum(b - 1, 0)
        q_rows = pl.ds(pl.multiple_of(b * blk, blk), blk)
        k_rows = pl.ds(pl.multiple_of(kb * blk, blk), 2 * blk)
        qb = load(0, res, q_rows)
        zero = jnp.zeros_like(qb)
        qs = jnp.concatenate([jnp.where(is_head[h], qb, zero) for h in range(DIL_HEADS)], axis=0)
        s = lax.dot_general(qs, load(1, res, k_rows), (((1,), (1,)), ((), ())), preferred_element_type=F32)
        s = s + bias_buf[jnp.minimum(b, 1)]
        m = jnp.max(s, axis=-1, keepdims=True)
        p = jnp.exp2(s - m)
        l = jnp.sum(p, axis=-1, keepdims=True)
        o = jnp.dot(p.astype(BF16), load(2, res, k_rows), preferred_element_type=F32)
        o_tok, m_tok, l_tok = fold(o), fold(m), fold(l)
        o_tok = o_tok / l_tok
        lse = m_tok + jnp.log2(l_tok)
        if prev_dilation is not None:
            step = dilation // prev_dilation
            p_rows = pl.ds(b * (blk * step) + res // prev_dilation, blk, stride=step)
            p_res = res % prev_dilation
            prev_o = jnp.concatenate([prev_o_ref[0, c, p_res, p_rows, :] for c in range(n_split)], axis=1)
            prev_lse = jnp.concatenate([prev_lse_ref[0, c, p_res, p_rows, :] for c in range(n_split)], axis=1)
            top = jnp.maximum(prev_lse, lse)
            w_prev = jnp.exp2(prev_lse - top)
            w_new = jnp.exp2(lse - top)
            den = w_prev + w_new
            o_tok = (w_prev * prev_o + w_new * o_tok) / den
            lse = top + jnp.log2(den)
        if last:
            seq_rows = pl.ds(b * (blk * dilation) + res, blk, stride=dilation)
            for c in range(n_split):
                o_ref[0, c, seq_rows, :] = o_tok[:, c * LANES:(c + 1) * LANES]
        else:
            for c in range(n_split):
                o_ref[0, c, res, q_rows, :] = o_tok[:, c * LANES:(c + 1) * LANES]
                lse_ref[0, c, res, q_rows, :] = lse[:, c * LANES:(c + 1) * LANES]
        return carry

    lax.fori_loop(0, dilation * nb, body, 0, unroll=8)


def _dilated_attention(qkv_groups):
    def batch_block(shape):
        return pl.BlockSpec((1,) + shape[1:], lambda b: (b,) + (0,) * (len(shape) - 1))

    merged, prev_d = (), None
    for g, qkv in enumerate(qkv_groups):
        B, _, n_split, d, L, _ = qkv.shape
        assert DIL_PATTERNS[g] == (ATTN_BLOCK * d, d)
        last = g == len(qkv_groups) - 1
        if last:
            out_shape = [jax.ShapeDtypeStruct((B, n_split, d * L, LANES), F32)]
        else:
            out_shape = [jax.ShapeDtypeStruct((B, n_split, d, L, LANES), F32)] * 2
        ins = (qkv,) + tuple(merged)
        merged = pl.pallas_call(
            functools.partial(_attn_kernel, dilation=d, prev_dilation=prev_d, last=last, seq=d * L),
            grid=(B,),
            in_specs=[batch_block(a.shape) for a in ins],
            out_specs=[batch_block(s.shape) for s in out_shape],
            out_shape=out_shape,
            scratch_shapes=[pltpu.VMEM((2, DIL_HEADS * ATTN_BLOCK, 2 * ATTN_BLOCK), F32)],
            compiler_params=pltpu.CompilerParams(
                dimension_semantics=("arbitrary",), vmem_limit_bytes=VMEM_LIMIT),
            name=f"dilated_attention_d{d}",
        )(*ins)
        prev_d = d
    return merged[0]


def _block_diag(w):
    H, n, _ = w.shape
    eye = jnp.eye(H, dtype=w.dtype)
    return (eye[:, None, :, None] * w[:, :, None, :]).reshape(H * n, H * n)


def kernel(x, mix_norm_g, ffn_norm_g, ev_w_in, ev_sc_conv_w, ev_cf_conv_w, ev_cf_conv_b, ev_cf_ln_g, ev_cf_ln_b, ev_w_out, od_w_in, od_lru_conv_w, od_lru_conv_b, od_lru_wa, od_lru_ba, od_lru_wx, od_lru_bx, od_lru_lam, od_q_norm_g, od_k_norm_g, od_w_out, ffn_w_gate, ffn_w_up, ffn_conv_w, ffn_conv_b, ffn_w_down):
    row = lambda p: p.reshape(1, -1)
    bf = lambda w: w.astype(BF16)
    n_heads = DIL_QKV // DIL_HEAD_DIM
    head_of = jnp.arange(DIL_OUT) // DIL_HEAD_DIM
    seg = (head_of[:, None] == head_of[None, :]).astype(BF16)

    ffn_wg, ffn_wu, ffn_wd = bf(ffn_w_gate), bf(ffn_w_up), bf(ffn_w_down)

    def ffn(x, layer, mix=None):
        return _conv_ffn(x, layer, row(ffn_norm_g[layer]), ffn_wg, ffn_wu,
                         _rep8(ffn_conv_w[layer]), _rep8(ffn_conv_b[layer]), ffn_wd, mix=mix)

    x = _even_mixer(x, row(mix_norm_g[0]), bf(ev_w_in[0]), _rep8(ev_sc_conv_w[0]), _rep8(ev_cf_conv_w[0]),
                    _rep8(ev_cf_conv_b[0]), _rep8(ev_cf_ln_g[0]), _rep8(ev_cf_ln_b[0]), bf(ev_w_out[0]))
    x = ffn(x, 0)
    y_lru, *qkv_groups = _odd_in(
        x, row(mix_norm_g[1]), bf(od_w_in[0]), _rep8(od_lru_conv_w[0]), _rep8(od_lru_conv_b[0]),
        bf(_block_diag(od_lru_wa[0])), _rep8(od_lru_ba[0]), bf(_block_diag(od_lru_wx[0])), _rep8(od_lru_bx[0]),
        row(od_lru_lam[0]), _rep8(jnp.tile(od_q_norm_g[0], n_heads)), _rep8(jnp.tile(od_k_norm_g[0], n_heads)), seg)
    y_att = _dilated_attention(qkv_groups)
    return ffn(x, 1, mix=(y_lru, y_att, bf(od_w_out[0])))
```

```python
import functools
import math

import jax
import jax.numpy as jnp
from jax import lax
from jax.experimental import pallas as pl
from jax.experimental.pallas import tpu as pltpu

F32 = jnp.float32
BF16 = jnp.bfloat16

D_MODEL = 1024
SC_WIDTH = 512
SC_TAPS = 3
CF_WIDTH = 512
CF_TAPS = 31
LRU_WIDTH = 512
LRU_TAPS = 4
LRU_C = 8.0
MXU_WIDTH = 256
GATE_BLOCK = MXU_WIDTH
DIL_PATTERNS = ((128, 1), (512, 4), (2048, 16))
DIL_HEADS = 4
DIL_HEAD_DIM = 64
DIL_OUT = DIL_HEADS * DIL_HEAD_DIM
DIL_QKV = len(DIL_PATTERNS) * DIL_OUT
ATTN_BLOCK = 128
D_FF = 2816
EPS = 1e-6

SUBLANES = 8
LANES = 128
SEQ_TILE = 512
CF_HALO = 32
ROW_CHUNK = 32
FFN_COL_CHUNK = 256
VMEM_LIMIT = 56 * 1024 * 1024


def _rms_bf16(x, g):
    y = x * lax.rsqrt(jnp.mean(x * x, axis=-1, keepdims=True) + EPS)
    return (y * g).astype(BF16)


def _sigmoid(x):
    return 1.0 / (1.0 + jnp.exp(-x))


def _rep8(p):
    return jnp.repeat(p.reshape(-1, p.shape[-1]), SUBLANES, axis=0)


def _bcast_rows(ref, k, cols, n):
    w = ref[k * SUBLANES:(k + 1) * SUBLANES, cols]
    return jnp.concatenate([w] * (n // SUBLANES), axis=0)


def _load_lanes(ref, start, n):
    return jnp.concatenate([ref[c, start:start + n, :] for c in range(ref.shape[0])], axis=1)


def _lane_conv(ref, w_ref, n_taps, start, n):
    out = []
    for c in range(ref.shape[0]):
        cols = slice(c * LANES, (c + 1) * LANES)
        acc = None
        for k in range(n_taps):
            term = _bcast_rows(w_ref, k, cols, n) * ref[c, start + k:start + k + n, :]
            acc = term if acc is None else acc + term
        out.append(acc)
    return jnp.concatenate(out, axis=1)


def _store_lanes(ref, start, val):
    for c in range(ref.shape[0]):
        ref[c, start:start + val.shape[0], :] = val[:, c * LANES:(c + 1) * LANES]


def _const_spec(shape):
    return pl.BlockSpec(shape, lambda *_: (0,) * len(shape), pipeline_mode=pl.Buffered(1))


def _tile_spec(width, ts):
    return pl.BlockSpec((1, ts, width), lambda b, j: (b, j, 0))


def _lane_split_spec(width, ts):
    return pl.BlockSpec((1, width // LANES, ts, LANES), lambda b, j: (b, 0, j, 0))


def _even_kernel(x_ref, g_ref, win_ref, scw_ref, cfw_ref, cfb_ref, lng_ref, lnb_ref, wout_ref,
                 o_ref, h_buf, z_buf, cx_buf, u_buf, cat_buf, *, ts):
    @pl.when(pl.program_id(1) == 0)
    def _():
        cx_buf[:, 0:SUBLANES, :] = jnp.zeros((SC_WIDTH // LANES, SUBLANES, LANES), F32)
        u_buf[:, 0:CF_HALO, :] = jnp.zeros((CF_WIDTH // LANES, CF_HALO, LANES), F32)

    sc_b, sc_c, sc_x, cf_a, cf_g = range(5)
    all_cols = slice(None)
    half, quarter = ts // 2, ts // 4
    n_chunks = ts // ROW_CHUNK
    per_quarter = n_chunks // 4
    h_buf[...] = _rms_bf16(x_ref[0], g_ref[...])

    def proj_rows(c, h):
        rows = slice(h * half, (h + 1) * half)
        z_buf[c, rows, :] = jnp.dot(h_buf[rows, :], win_ref[:, c * SC_WIDTH:(c + 1) * SC_WIDTH],
                                    preferred_element_type=F32)

    def proj_cols(c, h):
        cols = slice(h * MXU_WIDTH, (h + 1) * MXU_WIDTH)
        z_buf[c, :, cols] = jnp.dot(h_buf[...], win_ref[:, c * SC_WIDTH + cols.start:c * SC_WIDTH + cols.stop],
                                    preferred_element_type=F32)

    def stage_u(k):
        rows = slice(k * ROW_CHUNK, (k + 1) * ROW_CHUNK)
        _store_lanes(u_buf, CF_HALO + rows.start, z_buf[cf_a, rows, :] * _sigmoid(z_buf[cf_g, rows, :]))

    def stage_cx(k):
        rows = slice(k * ROW_CHUNK, (k + 1) * ROW_CHUNK)
        _store_lanes(cx_buf, SUBLANES + rows.start, z_buf[sc_c, rows, :] * z_buf[sc_x, rows, :])

    def conv_sc(k):
        r = k * ROW_CHUNK
        acc = _lane_conv(cx_buf, scw_ref, SC_TAPS, SUBLANES + r - (SC_TAPS - 1), ROW_CHUNK)
        cat_buf[r:r + ROW_CHUNK, 0:SC_WIDTH] = (z_buf[sc_b, r:r + ROW_CHUNK, :] * acc).astype(BF16)

    def conv_cf(k):
        r = k * ROW_CHUNK
        u = _lane_conv(u_buf, cfw_ref, CF_TAPS, CF_HALO + r - (CF_TAPS - 1), ROW_CHUNK)
        u += _bcast_rows(cfb_ref, 0, all_cols, ROW_CHUNK)
        mu = jnp.mean(u, axis=-1, keepdims=True)
        uc = u - mu
        var = jnp.mean(uc * uc, axis=-1, keepdims=True)
        y = (uc * lax.rsqrt(var + EPS) * _bcast_rows(lng_ref, 0, all_cols, ROW_CHUNK)
             + _bcast_rows(lnb_ref, 0, all_cols, ROW_CHUNK))
        cat_buf[r:r + ROW_CHUNK, SC_WIDTH:SC_WIDTH + CF_WIDTH] = (y * _sigmoid(y)).astype(BF16)

    def out_proj(q):
        rows = slice(q * quarter, (q + 1) * quarter)
        for k in range(q * per_quarter, (q + 1) * per_quarter):
            conv_sc(k)
        o_ref[0, rows, :] = x_ref[0, rows, :] + jnp.dot(
            cat_buf[rows, :], wout_ref[...], preferred_element_type=F32)

    proj_rows(cf_a, 0)
    proj_rows(cf_g, 0)
    for k in range(n_chunks // 2):
        stage_u(k)
    after_chunk = {
        0: [lambda: proj_rows(cf_a, 1)],
        1: [lambda: proj_rows(cf_g, 1)] + [functools.partial(stage_u, k) for k in range(n_chunks // 2, n_chunks)],
        2: [lambda: proj_cols(sc_c, 0)], 3: [lambda: proj_cols(sc_c, 1)],
        4: [lambda: proj_cols(sc_x, 0)], 5: [lambda: proj_cols(sc_x, 1)],
        6: [lambda: proj_cols(sc_b, 0)], 7: [lambda: proj_cols(sc_b, 1)],
        8: [functools.partial(stage_cx, k) for k in range(n_chunks)],
        9: [lambda: out_proj(0)], 10: [lambda: out_proj(1)], 12: [lambda: out_proj(2)], 15: [lambda: out_proj(3)],
    }
    assert n_chunks == 16
    for k in range(n_chunks):
        conv_cf(k)
        for piece in after_chunk.get(k, ()):
            piece()

    cx_buf[:, 0:SUBLANES, :] = cx_buf[:, ts:ts + SUBLANES, :]
    u_buf[:, 0:CF_HALO, :] = u_buf[:, ts:ts + CF_HALO, :]


def _even_mixer(x, g, w_in, sc_w, cf_w, cf_b, ln_g, ln_b, w_out, ts=SEQ_TILE):
    B, S, D = x.shape
    ev_in = w_in.shape[1]
    return pl.pallas_call(
        functools.partial(_even_kernel, ts=ts),
        grid=(B, S // ts),
        in_specs=[
            _tile_spec(D, ts),
            _const_spec((1, D)),
            _const_spec((D, ev_in)),
            _const_spec(sc_w.shape),
            _const_spec(cf_w.shape),
            _const_spec(cf_b.shape),
            _const_spec(ln_g.shape),
            _const_spec(ln_b.shape),
            _const_spec(w_out.shape),
        ],
        out_specs=_tile_spec(D, ts),
        out_shape=jax.ShapeDtypeStruct((B, S, D), F32),
        scratch_shapes=[
            pltpu.VMEM((ts, D), BF16),
            pltpu.VMEM((ev_in // SC_WIDTH, ts, SC_WIDTH), F32),
            pltpu.VMEM((SC_WIDTH // LANES, ts + SUBLANES, LANES), F32),
            pltpu.VMEM((CF_WIDTH // LANES, ts + CF_HALO, LANES), F32),
            pltpu.VMEM((ts, SC_WIDTH + CF_WIDTH), BF16),
        ],
        compiler_params=pltpu.CompilerParams(
            dimension_semantics=("arbitrary", "arbitrary"), vmem_limit_bytes=VMEM_LIMIT),
        name="even_mixer",
    )(x, g, w_in, sc_w, cf_w, cf_b, ln_g, ln_b, w_out)


def _ffn_body(x, ng_ref, wg_ref, wu_ref, cw_ref, cb_ref, wd_ref, o_ref, h_buf, g_buf, up_buf, act_buf, ts):
    @pl.when(pl.program_id(1) == 0)
    def _():
        g_buf[:, 0:SUBLANES, :] = jnp.zeros((D_FF // LANES, SUBLANES, LANES), F32)

    def g_rows(c0, start):
        return jnp.concatenate([g_buf[(c0 + l) // LANES, start:start + ROW_CHUNK, :]
                                for l in range(0, FFN_COL_CHUNK, LANES)], axis=1)

    h_buf[...] = _rms_bf16(x, ng_ref[...])
    for c0 in range(0, D_FF, FFN_COL_CHUNK):
        cols = slice(c0, c0 + FFN_COL_CHUNK)
        g = jnp.dot(h_buf[...], wg_ref[:, cols], preferred_element_type=F32)
        for l in range(0, FFN_COL_CHUNK, LANES):
            g_buf[(c0 + l) // LANES, SUBLANES:SUBLANES + ts, :] = g[:, l:l + LANES]
        up_buf[:, cols] = jnp.dot(h_buf[...], wu_ref[:, cols], preferred_element_type=F32)
        w0, w1, w2 = (_bcast_rows(cw_ref, k, cols, ROW_CHUNK) for k in range(3))
        bias = _bcast_rows(cb_ref, 0, cols, ROW_CHUNK)
        for r in range(0, ts, ROW_CHUNK):
            gc = (w0 * g_rows(c0, r + SUBLANES - 2) + w1 * g_rows(c0, r + SUBLANES - 1)
                  + w2 * g_rows(c0, r + SUBLANES) + bias)
            act_buf[r:r + ROW_CHUNK, cols] = (gc * _sigmoid(gc) * up_buf[r:r + ROW_CHUNK, cols]).astype(BF16)
    o_ref[0] = x + jnp.dot(act_buf[...], wd_ref[...], preferred_element_type=F32)
    g_buf[:, 0:SUBLANES, :] = g_buf[:, ts:ts + SUBLANES, :]


def _ffn_kernel(x_ref, ng_ref, wg_ref, wu_ref, cw_ref, cb_ref, wd_ref, o_ref,
                *scratch, ts):
    _ffn_body(x_ref[0], ng_ref, wg_ref, wu_ref, cw_ref, cb_ref, wd_ref, o_ref, *scratch, ts)


def _proj_ffn_kernel(x_ref, ya_ref, yb_ref, wo_ref, ng_ref, wg_ref, wu_ref, cw_ref, cb_ref, wd_ref, o_ref,
                     *scratch, ts):
    wa = ya_ref.shape[-1]
    yb = jnp.concatenate([yb_ref[0, c] for c in range(yb_ref.shape[1])], axis=1)
    x = x_ref[0] + jnp.dot(ya_ref[0].astype(BF16), wo_ref[0:wa, :], preferred_element_type=F32)
    x = x + jnp.dot(yb.astype(BF16), wo_ref[wa:, :], preferred_element_type=F32)
    _ffn_body(x, ng_ref, wg_ref, wu_ref, cw_ref, cb_ref, wd_ref, o_ref, *scratch, ts)


def _layer_spec(stacked, layer):
    zeros = (0,) * (stacked.ndim - 1)
    return pl.BlockSpec((None,) + stacked.shape[1:], lambda *_: (layer,) + zeros, pipeline_mode=pl.Buffered(1))


def _conv_ffn(x, layer, ng, wg, wu, cw, cb, wd, mix=None, ts=SEQ_TILE):
    B, S, D = x.shape
    ffn_specs = [
        _const_spec((1, D)),
        _layer_spec(wg, layer),
        _layer_spec(wu, layer),
        _const_spec(cw.shape),
        _const_spec(cb.shape),
        _layer_spec(wd, layer),
    ]
    if mix is None:
        body, ins, specs = _ffn_kernel, (x,), [_tile_spec(D, ts)]
    else:
        ya, yb, wo = mix
        body, ins = _proj_ffn_kernel, (x, ya, yb, wo)
        specs = [_tile_spec(D, ts), _tile_spec(ya.shape[-1], ts), _lane_split_spec(yb.shape[1] * LANES, ts),
                 _const_spec(wo.shape)]
    return pl.pallas_call(
        functools.partial(body, ts=ts),
        grid=(B, S // ts),
        in_specs=specs + ffn_specs,
        out_specs=_tile_spec(D, ts),
        out_shape=jax.ShapeDtypeStruct((B, S, D), F32),
        scratch_shapes=[
            pltpu.VMEM((ts, D), BF16),
            pltpu.VMEM((D_FF // LANES, ts + SUBLANES, LANES), F32),
            pltpu.VMEM((ts, D_FF), F32),
            pltpu.VMEM((ts, D_FF), BF16),
        ],
        compiler_params=pltpu.CompilerParams(
            dimension_semantics=("arbitrary", "arbitrary"), vmem_limit_bytes=VMEM_LIMIT),
        name="conv_ffn" if mix is None else "proj_conv_ffn",
    )(*ins, ng, wg, wu, cw, cb, wd)


def _odd_in_kernel(x_ref, g_ref, win_ref, cw_ref, cb_ref, wa_ref, ba_ref, wx_ref, bx_ref, lam_ref,
                   qg_ref, kg_ref, seg_ref,
                   ylru_ref, qkv0_ref, qkv1_ref, qkv2_ref,
                   h_buf, xin_buf, xc_buf, xcb_buf, ga_buf, gx_buf, gate_buf, t_buf, nq_buf, a_buf, b_buf,
                   al_buf, bl_buf, hin_buf, h_carry, *, ts):
    qkv_refs = (qkv0_ref, qkv1_ref, qkv2_ref)
    @pl.when(pl.program_id(1) == 0)
    def _():
        xin_buf[:, 0:SUBLANES, :] = jnp.zeros((LRU_WIDTH // LANES, SUBLANES, LANES), F32)
        h_carry[...] = jnp.zeros((SUBLANES, LRU_WIDTH), F32)

    all_cols = slice(None)
    h_buf[...] = _rms_bf16(x_ref[0], g_ref[...])

    def proj(c0, width):
        return jnp.dot(h_buf[...], win_ref[:, c0:c0 + width], preferred_element_type=F32)

    def lru_input():
        _store_lanes(xin_buf, SUBLANES, proj(0, LRU_WIDTH))
        for r in range(0, ts, ROW_CHUNK):
            acc = _bcast_rows(cb_ref, 0, all_cols, ROW_CHUNK)
            for k in range(LRU_TAPS):
                acc += (_bcast_rows(cw_ref, k, all_cols, ROW_CHUNK)
                        * _load_lanes(xin_buf, SUBLANES + r - (LRU_TAPS - 1) + k, ROW_CHUNK))
            xc_buf[r:r + ROW_CHUNK, :] = acc
            xcb_buf[r:r + ROW_CHUNK, :] = acc.astype(BF16)
        xin_buf[:, 0:SUBLANES, :] = xin_buf[:, ts:ts + SUBLANES, :]

    def lru_gates():
        for w_ref, buf in ((wa_ref, ga_buf), (wx_ref, gx_buf)):
            for c0 in range(0, LRU_WIDTH, GATE_BLOCK):
                cols = slice(c0, c0 + GATE_BLOCK)
                buf[:, cols] = jnp.dot(xcb_buf[:, cols], w_ref[cols, cols], preferred_element_type=F32)

    def lru_gate_proj():
        gate_buf[...] = proj(LRU_WIDTH, LRU_WIDTH)

    nlam = -lam_ref[...]
    softplus = jnp.maximum(nlam, 0.0) + jnp.log(1.0 + jnp.exp(-jnp.abs(nlam)))
    decay = jnp.broadcast_to(-LRU_C * softplus, (SUBLANES, LRU_WIDTH))
    row = lax.broadcasted_iota(jnp.int32, (SUBLANES, LRU_WIDTH), 0)

    def scan_group(g):
        rows = slice(g * SUBLANES, (g + 1) * SUBLANES)
        rg = _sigmoid(ga_buf[rows, :] + ba_ref[...])
        ig = _sigmoid(gx_buf[rows, :] + bx_ref[...])
        a = jnp.exp(decay * rg)
        b = jnp.sqrt(1.0 - a * a) * (ig * xc_buf[rows, :])
        for s in (1, 2, 4):
            a_prev = jnp.where(row >= s, pltpu.roll(a, s, 0), 1.0)
            b_prev = jnp.where(row >= s, pltpu.roll(b, s, 0), 0.0)
            b = a * b_prev + b
            a = a * a_prev
        a_buf[rows, :] = a
        b_buf[rows, :] = b
        al_buf[rows, :] = jnp.broadcast_to(a[SUBLANES - 1:SUBLANES, :], (SUBLANES, LRU_WIDTH))
        bl_buf[rows, :] = jnp.broadcast_to(b[SUBLANES - 1:SUBLANES, :], (SUBLANES, LRU_WIDTH))

    half = ts // 2
    q0 = 2 * LRU_WIDTH
    qk_rows = 4 * ROW_CHUNK
    norm_args = ((qg_ref, DIL_HEAD_DIM ** -0.5 * math.log2(math.e)), (kg_ref, 1.0))
    v_index = len(norm_args)
    n_split = DIL_OUT // LANES

    def qkv_proj(which, h):
        rows = slice(h * half, (h + 1) * half)
        c0 = q0 + which * DIL_QKV
        z = jnp.dot(h_buf[rows, :], win_ref[:, c0:c0 + DIL_QKV], preferred_element_type=F32)
        if which == v_index:
            for c in range(DIL_QKV // LANES):
                nq_buf[which, c, rows, :] = z[:, c * LANES:(c + 1) * LANES]
        else:
            t_buf[which, rows, :] = z

    def qk_norm(which, h):
        gain_ref, scale = norm_args[which]
        for r in range(h * half, (h + 1) * half, qk_rows):
            for g0 in range(0, DIL_QKV, DIL_OUT):
                t = t_buf[which, r:r + qk_rows, g0:g0 + DIL_OUT]
                ss = jnp.dot((t * t).astype(BF16), seg_ref[...], preferred_element_type=F32)
                gain = _bcast_rows(gain_ref, 0, slice(g0, g0 + DIL_OUT), qk_rows) * scale
                t = t * lax.rsqrt(ss * (1.0 / DIL_HEAD_DIM) + EPS) * gain
                for l in range(0, DIL_OUT, LANES):
                    nq_buf[which, (g0 + l) // LANES, r:r + qk_rows, :] = t[:, l:l + LANES]

    def scatter(which, h):
        for g, (_, d) in enumerate(DIL_PATTERNS):
            n = half // d
            for c in range(n_split):
                for r in range(d):
                    rows = slice(h * half, (h + 1) * half) if d == 1 else pl.ds(h * half + r, n, stride=d)
                    qkv_refs[g][0, which, c, r, h * n:(h + 1) * n, :] = (
                        nq_buf[which, g * n_split + c, rows, :].astype(BF16))

    lru_input()
    qkv_proj(0, 0)
    lru_gates()
    qkv_proj(1, 0)
    lru_gate_proj()
    qk_norm(0, 0)
    qkv_proj(v_index, 0)
    scatter(0, 0)
    qkv_proj(0, 1)
    qk_norm(1, 0)
    qkv_proj(1, 1)
    scatter(1, 0)
    scatter(v_index, 0)
    qkv_proj(v_index, 1)
    epilogues = {8: lambda: qk_norm(0, 1), 20: lambda: scatter(0, 1), 32: lambda: qk_norm(1, 1),
                 44: lambda: scatter(1, 1), 56: lambda: scatter(v_index, 1)}
    for g in range(ts // SUBLANES):
        scan_group(g)
        if g in epilogues:
            epilogues[g]()

    hc = h_carry[...]
    for r in range(0, ts, SUBLANES):
        rows = slice(r, r + SUBLANES)
        hin_buf[rows, :] = hc
        hc = al_buf[rows, :] * hc + bl_buf[rows, :]
    h_carry[...] = hc
    for r in range(0, ts, ROW_CHUNK):
        rows = slice(r, r + ROW_CHUNK)
        hs = a_buf[rows, :] * hin_buf[rows, :] + b_buf[rows, :]
        gate = gate_buf[rows, :]
        gelu = 0.5 * gate * (1.0 + jnp.tanh(math.sqrt(2.0 / math.pi) * (gate + 0.044715 * (gate * gate * gate))))
        ylru_ref[0, rows, :] = hs * gelu


def _odd_in(x, g, w_in, cw, cb, wa, ba, wx, bx, lam, qg, kg, seg, ts=SEQ_TILE):
    B, S, D = x.shape
    n_split = DIL_OUT // LANES
    return pl.pallas_call(
        functools.partial(_odd_in_kernel, ts=ts),
        grid=(B, S // ts),
        in_specs=[
            _tile_spec(D, ts),
            _const_spec((1, D)),
            _const_spec(w_in.shape),
            _const_spec(cw.shape),
            _const_spec(cb.shape),
            _const_spec(wa.shape),
            _const_spec(ba.shape),
            _const_spec(wx.shape),
            _const_spec(bx.shape),
            _const_spec((1, LRU_WIDTH)),
            _const_spec(qg.shape),
            _const_spec(kg.shape),
            _const_spec(seg.shape),
        ],
        out_specs=[_tile_spec(LRU_WIDTH, ts)] + [
            pl.BlockSpec((1, 3, n_split, d, ts // d, LANES), lambda b, j: (b, 0, 0, 0, j, 0))
            for _, d in DIL_PATTERNS],
        out_shape=[jax.ShapeDtypeStruct((B, S, LRU_WIDTH), F32)] + [
            jax.ShapeDtypeStruct((B, 3, n_split, d, S // d, LANES), BF16) for _, d in DIL_PATTERNS],
        scratch_shapes=[
            pltpu.VMEM((ts, D), BF16),
            pltpu.VMEM((LRU_WIDTH // LANES, ts + SUBLANES, LANES), F32),
            pltpu.VMEM((ts, LRU_WIDTH), F32),
            pltpu.VMEM((ts, LRU_WIDTH), BF16),
            pltpu.VMEM((ts, LRU_WIDTH), F32),
            pltpu.VMEM((ts, LRU_WIDTH), F32),
            pltpu.VMEM((ts, LRU_WIDTH), F32),
            pltpu.VMEM((2, ts, DIL_QKV), F32),
            pltpu.VMEM((3, DIL_QKV // LANES, ts, LANES), F32),
            pltpu.VMEM((ts, LRU_WIDTH), F32),
            pltpu.VMEM((ts, LRU_WIDTH), F32),
            pltpu.VMEM((ts, LRU_WIDTH), F32),
            pltpu.VMEM((ts, LRU_WIDTH), F32),
            pltpu.VMEM((ts, LRU_WIDTH), F32),
            pltpu.VMEM((SUBLANES, LRU_WIDTH), F32),
        ],
        compiler_params=pltpu.CompilerParams(
            dimension_semantics=("arbitrary", "arbitrary"), vmem_limit_bytes=VMEM_LIMIT),
        name="odd_in_lru",
    )(x, g, w_in, cw, cb, wa, ba, wx, bx, lam, qg, kg, seg)


def _attn_kernel(*refs, dilation, prev_dilation, last, seq):
    qkv_ref, refs = refs[0], refs[1:]
    if prev_dilation is not None:
        (prev_o_ref, prev_lse_ref), refs = refs[:2], refs[2:]
    if last:
        o_ref, bias_buf = refs
    else:
        o_ref, lse_ref, bias_buf = refs
    blk = ATTN_BLOCK
    nb = seq // dilation // blk
    n_split = DIL_OUT // LANES
    qi = lax.broadcasted_iota(jnp.int32, (DIL_HEADS * blk, 2 * blk), 0) % blk
    kj = lax.broadcasted_iota(jnp.int32, (DIL_HEADS * blk, 2 * blk), 1)
    for t in range(2):
        steps = qi - kj + t * blk
        bias_buf[t] = jnp.where((steps >= 0) & (steps <= blk), 0.0, -jnp.inf).astype(F32)

    lane_head = lax.broadcasted_iota(jnp.int32, (1, DIL_OUT), 1) // DIL_HEAD_DIM
    is_head = [lane_head == h for h in range(DIL_HEADS)]

    def fold(t):
        out = t[(DIL_HEADS - 1) * blk:DIL_HEADS * blk]
        for h in range(DIL_HEADS - 2, -1, -1):
            out = jnp.where(is_head[h], t[h * blk:(h + 1) * blk], out)
        return out

    def load(which, res, rows):
        return jnp.concatenate([qkv_ref[0, which, c, res, rows, :] for c in range(n_split)], axis=1)

    def body(idx, carry):
        res = idx // nb
        b = idx % nb
        kb = jnp.maximum(b - 1, 0)
        q_rows = pl.ds(pl.multiple_of(b * blk, blk), blk)
        k_rows = pl.ds(pl.multiple_of(kb * blk, blk), 2 * blk)
        qb = load(0, res, q_rows)
        zero = jnp.zeros_like(qb)
        qs = jnp.concatenate([jnp.where(is_head[h], qb, zero) for h in range(DIL_HEADS)], axis=0)
        s = lax.dot_general(qs, load(1, res, k_rows), (((1,), (1,)), ((), ())), preferred_element_type=F32)
        s = s + bias_buf[jnp.minimum(b, 1)]
        m = jnp.max(s, axis=-1, keepdims=True)
        p = jnp.exp2(s - m)
        l = jnp.sum(p, axis=-1, keepdims=True)
        o = jnp.dot(p.astype(BF16), load(2, res, k_rows), preferred_element_type=F32)
        o_tok, m_tok, l_tok = fold(o), fold(m), fold(l)
        o_tok = o_tok / l_tok
        lse = m_tok + jnp.log2(l_tok)
        if prev_dilation is not None:
            step = dilation // prev_dilation
            p_rows = pl.ds(b * (blk * step) + res // prev_dilation, blk, stride=step)
            p_res = res % prev_dilation
            prev_o = jnp.concatenate([prev_o_ref[0, c, p_res, p_rows, :] for c in range(n_split)], axis=1)
            prev_lse = jnp.concatenate([prev_lse_ref[0, c, p_res, p_rows, :] for c in range(n_split)], axis=1)
            top = jnp.maximum(prev_lse, lse)
            w_prev = jnp.exp2(prev_lse - top)
            w_new = jnp.exp2(lse - top)
            den = w_prev + w_new
            o_tok = (w_prev * prev_o + w_new * o_tok) / den
            lse = top + jnp.log2(den)
        if last:
            seq_rows = pl.ds(b * (blk * dilation) + res, blk, stride=dilation)
            for c in range(n_split):
                o_ref[0, c, seq_rows, :] = o_tok[:, c * LANES:(c + 1) * LANES]
        else:
            for c in range(n_split):
                o_ref[0, c, res, q_rows, :] = o_tok[:, c * LANES:(c + 1) * LANES]
                lse_ref[0, c, res, q_rows, :] = lse[:, c * LANES:(c + 1) * LANES]
        return carry

    lax.fori_loop(0, dilation * nb, body, 0, unroll=8)


def _dilated_attention(qkv_groups):
    def batch_block(shape):
        return pl.BlockSpec((1,) + shape[1:], lambda b: (b,) + (0,) * (len(shape) - 1))

    merged, prev_d = (), None
    for g, qkv in enumerate(qkv_groups):
        B, _, n_split, d, L, _ = qkv.shape
        assert DIL_PATTERNS[g] == (ATTN_BLOCK * d, d)
        last = g == len(qkv_groups) - 1
        if last:
            out_shape = [jax.ShapeDtypeStruct((B, n_split, d * L, LANES), F32)]
        else:
            out_shape = [jax.ShapeDtypeStruct((B, n_split, d, L, LANES), F32)] * 2
        ins = (qkv,) + tuple(merged)
        merged = pl.pallas_call(
            functools.partial(_attn_kernel, dilation=d, prev_dilation=prev_d, last=last, seq=d * L),
            grid=(B,),
            in_specs=[batch_block(a.shape) for a in ins],
            out_specs=[batch_block(s.shape) for s in out_shape],
            out_shape=out_shape,
            scratch_shapes=[pltpu.VMEM((2, DIL_HEADS * ATTN_BLOCK, 2 * ATTN_BLOCK), F32)],
            compiler_params=pltpu.CompilerParams(
                dimension_semantics=("arbitrary",), vmem_limit_bytes=VMEM_LIMIT),
            name=f"dilated_attention_d{d}",
        )(*ins)
        prev_d = d
    return merged[0]


def _block_diag(w):
    H, n, _ = w.shape
    eye = jnp.eye(H, dtype=w.dtype)
    return (eye[:, None, :, None] * w[:, :, None, :]).reshape(H * n, H * n)


def kernel(x, mix_norm_g, ffn_norm_g, ev_w_in, ev_sc_conv_w, ev_cf_conv_w, ev_cf_conv_b, ev_cf_ln_g, ev_cf_ln_b, ev_w_out, od_w_in, od_lru_conv_w, od_lru_conv_b, od_lru_wa, od_lru_ba, od_lru_wx, od_lru_bx, od_lru_lam, od_q_norm_g, od_k_norm_g, od_w_out, ffn_w_gate, ffn_w_up, ffn_conv_w, ffn_conv_b, ffn_w_down):
    row = lambda p: p.reshape(1, -1)
    bf = lambda w: w.astype(BF16)
    n_heads = DIL_QKV // DIL_HEAD_DIM
    head_of = jnp.arange(DIL_OUT) // DIL_HEAD_DIM
    seg = (head_of[:, None] == head_of[None, :]).astype(BF16)

    ffn_wg, ffn_wu, ffn_wd = bf(ffn_w_gate), bf(ffn_w_up), bf(ffn_w_down)

    def ffn(x, layer, mix=None):
        return _conv_ffn(x, layer, row(ffn_norm_g[layer]), ffn_wg, ffn_wu,
                         _rep8(ffn_conv_w[layer]), _rep8(ffn_conv_b[layer]), ffn_wd, mix=mix)

    x = _even_mixer(x, row(mix_norm_g[0]), bf(ev_w_in[0]), _rep8(ev_sc_conv_w[0]), _rep8(ev_cf_conv_w[0]),
                    _rep8(ev_cf_conv_b[0]), _rep8(ev_cf_ln_g[0]), _rep8(ev_cf_ln_b[0]), bf(ev_w_out[0]))
    x = ffn(x, 0)
    y_lru, *qkv_groups = _odd_in(
        x, row(mix_norm_g[1]), bf(od_w_in[0]), _rep8(od_lru_conv_w[0]), _rep8(od_lru_conv_b[0]),
        bf(_block_diag(od_lru_wa[0])), _rep8(od_lru_ba[0]), bf(_block_diag(od_lru_wx[0])), _rep8(od_lru_bx[0]),
        row(od_lru_lam[0]), _rep8(jnp.tile(od_q_norm_g[0], n_heads)), _rep8(jnp.tile(od_k_norm_g[0], n_heads)), seg)
    y_att = _dilated_attention(qkv_groups)
    return ffn(x, 1, mix=(y_lru, y_att, bf(od_w_out[0])))
```

```python
import functools
import math

import jax
import jax.numpy as jnp
from jax import lax
from jax.experimental import pallas as pl
from jax.experimental.pallas import tpu as pltpu

F32 = jnp.float32
BF16 = jnp.bfloat16

D_MODEL = 1024
SC_WIDTH = 512
SC_TAPS = 3
CF_WIDTH = 512
CF_TAPS = 31
LRU_WIDTH = 512
LRU_TAPS = 4
LRU_C = 8.0
MXU_WIDTH = 256
GATE_BLOCK = MXU_WIDTH
DIL_PATTERNS = ((128, 1), (512, 4), (2048, 16))
DIL_HEADS = 4
DIL_HEAD_DIM = 64
DIL_OUT = DIL_HEADS * DIL_HEAD_DIM
DIL_QKV = len(DIL_PATTERNS) * DIL_OUT
ATTN_BLOCK = 128
D_FF = 2816
EPS = 1e-6

SUBLANES = 8
LANES = 128
SEQ_TILE = 512
CF_HALO = 32
ROW_CHUNK = 32
FFN_COL_CHUNK = 256
VMEM_LIMIT = 56 * 1024 * 1024


def _rms_bf16(x, g):
    y = x * lax.rsqrt(jnp.mean(x * x, axis=-1, keepdims=True) + EPS)
    return (y * g).astype(BF16)


def _sigmoid(x):
    return 1.0 / (1.0 + jnp.exp(-x))


def _rep8(p):
    return jnp.repeat(p.reshape(-1, p.shape[-1]), SUBLANES, axis=0)


def _bcast_rows(ref, k, cols, n):
    w = ref[k * SUBLANES:(k + 1) * SUBLANES, cols]
    return jnp.concatenate([w] * (n // SUBLANES), axis=0)


def _load_lanes(ref, start, n):
    return jnp.concatenate([ref[c, start:start + n, :] for c in range(ref.shape[0])], axis=1)


def _lane_conv(ref, w_ref, n_taps, start, n):
    out = []
    for c in range(ref.shape[0]):
        cols = slice(c * LANES, (c + 1) * LANES)
        acc = None
        for k in range(n_taps):
            term = _bcast_rows(w_ref, k, cols, n) * ref[c, start + k:start + k + n, :]
            acc = term if acc is None else acc + term
        out.append(acc)
    return jnp.concatenate(out, axis=1)


def _store_lanes(ref, start, val):
    for c in range(ref.shape[0]):
        ref[c, start:start + val.shape[0], :] = val[:, c * LANES:(c + 1) * LANES]


def _const_spec(shape):
    return pl.BlockSpec(shape, lambda *_: (0,) * len(shape), pipeline_mode=pl.Buffered(1))


def _tile_spec(width, ts):
    return pl.BlockSpec((1, ts, width), lambda b, j: (b, j, 0))


def _lane_split_spec(width, ts):
    return pl.BlockSpec((1, width // LANES, ts, LANES), lambda b, j: (b, 0, j, 0))


def _even_kernel(x_ref, g_ref, win_ref, scw_ref, cfw_ref, cfb_ref, lng_ref, lnb_ref, wout_ref,
                 o_ref, h_buf, z_buf, cx_buf, u_buf, cat_buf, *, ts):
    @pl.when(pl.program_id(1) == 0)
    def _():
        cx_buf[:, 0:SUBLANES, :] = jnp.zeros((SC_WIDTH // LANES, SUBLANES, LANES), F32)
        u_buf[:, 0:CF_HALO, :] = jnp.zeros((CF_WIDTH // LANES, CF_HALO, LANES), F32)

    sc_b, sc_c, sc_x, cf_a, cf_g = range(5)
    all_cols = slice(None)
    half, quarter = ts // 2, ts // 4
    n_chunks = ts // ROW_CHUNK
    per_quarter = n_chunks // 4
    h_buf[...] = _rms_bf16(x_ref[0], g_ref[...])

    def proj_rows(c, h):
        rows = slice(h * half, (h + 1) * half)
        z_buf[c, rows, :] = jnp.dot(h_buf[rows, :], win_ref[:, c * SC_WIDTH:(c + 1) * SC_WIDTH],
                                    preferred_element_type=F32)

    def proj_cols(c, h):
        cols = slice(h * MXU_WIDTH, (h + 1) * MXU_WIDTH)
        z_buf[c, :, cols] = jnp.dot(h_buf[...], win_ref[:, c * SC_WIDTH + cols.start:c * SC_WIDTH + cols.stop],
                                    preferred_element_type=F32)

    def stage_u(k):
        rows = slice(k * ROW_CHUNK, (k + 1) * ROW_CHUNK)
        _store_lanes(u_buf, CF_HALO + rows.start, z_buf[cf_a, rows, :] * _sigmoid(z_buf[cf_g, rows, :]))

    def stage_cx(k):
        rows = slice(k * ROW_CHUNK, (k + 1) * ROW_CHUNK)
        _store_lanes(cx_buf, SUBLANES + rows.start, z_buf[sc_c, rows, :] * z_buf[sc_x, rows, :])

    def conv_sc(k):
        r = k * ROW_CHUNK
        acc = _lane_conv(cx_buf, scw_ref, SC_TAPS, SUBLANES + r - (SC_TAPS - 1), ROW_CHUNK)
        cat_buf[r:r + ROW_CHUNK, 0:SC_WIDTH] = (z_buf[sc_b, r:r + ROW_CHUNK, :] * acc).astype(BF16)

    def conv_cf(k):
        r = k * ROW_CHUNK
        u = _lane_conv(u_buf, cfw_ref, CF_TAPS, CF_HALO + r - (CF_TAPS - 1), ROW_CHUNK)
        u += _bcast_rows(cfb_ref, 0, all_cols, ROW_CHUNK)
        mu = jnp.mean(u, axis=-1, keepdims=True)
        uc = u - mu
        var = jnp.mean(uc * uc, axis=-1, keepdims=True)
        y = (uc * lax.rsqrt(var + EPS) * _bcast_rows(lng_ref, 0, all_cols, ROW_CHUNK)
             + _bcast_rows(lnb_ref, 0, all_cols, ROW_CHUNK))
        cat_buf[r:r + ROW_CHUNK, SC_WIDTH:SC_WIDTH + CF_WIDTH] = (y * _sigmoid(y)).astype(BF16)

    def out_proj(q):
        rows = slice(q * quarter, (q + 1) * quarter)
        for k in range(q * per_quarter, (q + 1) * per_quarter):
            conv_sc(k)
        o_ref[0, rows, :] = x_ref[0, rows, :] + jnp.dot(
            cat_buf[rows, :], wout_ref[...], preferred_element_type=F32)

    proj_rows(cf_a, 0)
    proj_rows(cf_g, 0)
    for k in range(n_chunks // 2):
        stage_u(k)
    after_chunk = {
        0: [lambda: proj_rows(cf_a, 1)],
        1: [lambda: proj_rows(cf_g, 1)] + [functools.partial(stage_u, k) for k in range(n_chunks // 2, n_chunks)],
        2: [lambda: proj_cols(sc_c, 0)], 3: [lambda: proj_cols(sc_c, 1)],
        4: [lambda: proj_cols(sc_x, 0)], 5: [lambda: proj_cols(sc_x, 1)],
        6: [lambda: proj_cols(sc_b, 0)], 7: [lambda: proj_cols(sc_b, 1)],
        8: [functools.partial(stage_cx, k) for k in range(n_chunks)],
        9: [lambda: out_proj(0)], 10: [lambda: out_proj(1)], 12: [lambda: out_proj(2)], 15: [lambda: out_proj(3)],
    }
    assert n_chunks == 16
    for k in range(n_chunks):
        conv_cf(k)
        for piece in after_chunk.get(k, ()):
            piece()

    cx_buf[:, 0:SUBLANES, :] = cx_buf[:, ts:ts + SUBLANES, :]
    u_buf[:, 0:CF_HALO, :] = u_buf[:, ts:ts + CF_HALO, :]


def _even_mixer(x, g, w_in, sc_w, cf_w, cf_b, ln_g, ln_b, w_out, ts=SEQ_TILE):
    B, S, D = x.shape
    ev_in = w_in.shape[1]
    return pl.pallas_call(
        functools.partial(_even_kernel, ts=ts),
        grid=(B, S // ts),
        in_specs=[
            _tile_spec(D, ts),
            _const_spec((1, D)),
            _const_spec((D, ev_in)),
            _const_spec(sc_w.shape),
            _const_spec(cf_w.shape),
            _const_spec(cf_b.shape),
            _const_spec(ln_g.shape),
            _const_spec(ln_b.shape),
            _const_spec(w_out.shape),
        ],
        out_specs=_tile_spec(D, ts),
        out_shape=jax.ShapeDtypeStruct((B, S, D), F32),
        scratch_shapes=[
            pltpu.VMEM((ts, D), BF16),
            pltpu.VMEM((ev_in // SC_WIDTH, ts, SC_WIDTH), F32),
            pltpu.VMEM((SC_WIDTH // LANES, ts + SUBLANES, LANES), F32),
            pltpu.VMEM((CF_WIDTH // LANES, ts + CF_HALO, LANES), F32),
            pltpu.VMEM((ts, SC_WIDTH + CF_WIDTH), BF16),
        ],
        compiler_params=pltpu.CompilerParams(
            dimension_semantics=("arbitrary", "arbitrary"), vmem_limit_bytes=VMEM_LIMIT),
        name="even_mixer",
    )(x, g, w_in, sc_w, cf_w, cf_b, ln_g, ln_b, w_out)


def _ffn_body(x, wg_ref, wu_ref, cw_ref, cb_ref, wd_ref, o_ref, h_buf, r_buf, g_buf, up_buf, act_buf, ts):
    @pl.when(pl.program_id(1) == 0)
    def _():
        g_buf[:, 0:SUBLANES, :] = jnp.zeros((D_FF // LANES, SUBLANES, LANES), F32)

    def g_rows(c0, start):
        return jnp.concatenate([g_buf[(c0 + l) // LANES, start:start + ROW_CHUNK, :]
                                for l in range(0, FFN_COL_CHUNK, LANES)], axis=1)

    h_buf[...] = x.astype(BF16)
    r_buf[...] = jnp.broadcast_to(lax.rsqrt(jnp.mean(x * x, axis=-1, keepdims=True) + EPS), r_buf.shape)
    for c0 in range(0, D_FF, FFN_COL_CHUNK):
        cols = slice(c0, c0 + FFN_COL_CHUNK)
        scale = jnp.concatenate([r_buf[...]] * (FFN_COL_CHUNK // LANES), axis=1)
        g = jnp.dot(h_buf[...], wg_ref[:, cols], preferred_element_type=F32) * scale
        for l in range(0, FFN_COL_CHUNK, LANES):
            g_buf[(c0 + l) // LANES, SUBLANES:SUBLANES + ts, :] = g[:, l:l + LANES]
        up_buf[:, cols] = jnp.dot(h_buf[...], wu_ref[:, cols], preferred_element_type=F32) * scale
        w0, w1, w2 = (_bcast_rows(cw_ref, k, cols, ROW_CHUNK) for k in range(3))
        bias = _bcast_rows(cb_ref, 0, cols, ROW_CHUNK)
        for r in range(0, ts, ROW_CHUNK):
            gc = (w0 * g_rows(c0, r + SUBLANES - 2) + w1 * g_rows(c0, r + SUBLANES - 1)
                  + w2 * g_rows(c0, r + SUBLANES) + bias)
            act_buf[r:r + ROW_CHUNK, cols] = (gc * _sigmoid(gc) * up_buf[r:r + ROW_CHUNK, cols]).astype(BF16)
    o_ref[0] = x + jnp.dot(act_buf[...], wd_ref[...], preferred_element_type=F32)
    g_buf[:, 0:SUBLANES, :] = g_buf[:, ts:ts + SUBLANES, :]


def _ffn_kernel(x_ref, wg_ref, wu_ref, cw_ref, cb_ref, wd_ref, o_ref,
                *scratch, ts):
    _ffn_body(x_ref[0], wg_ref, wu_ref, cw_ref, cb_ref, wd_ref, o_ref, *scratch, ts)


def _proj_ffn_kernel(x_ref, ya_ref, yb_ref, wo_ref, wg_ref, wu_ref, cw_ref, cb_ref, wd_ref, o_ref,
                     *scratch, ts):
    wa = ya_ref.shape[-1]
    yb = jnp.concatenate([yb_ref[0, c] for c in range(yb_ref.shape[1])], axis=1)
    x = x_ref[0] + jnp.dot(ya_ref[0].astype(BF16), wo_ref[0:wa, :], preferred_element_type=F32)
    x = x + jnp.dot(yb.astype(BF16), wo_ref[wa:, :], preferred_element_type=F32)
    _ffn_body(x, wg_ref, wu_ref, cw_ref, cb_ref, wd_ref, o_ref, *scratch, ts)


def _layer_spec(stacked, layer):
    zeros = (0,) * (stacked.ndim - 1)
    return pl.BlockSpec((None,) + stacked.shape[1:], lambda *_: (layer,) + zeros, pipeline_mode=pl.Buffered(1))


def _conv_ffn(x, layer, wg, wu, cw, cb, wd, mix=None, ts=SEQ_TILE):
    B, S, D = x.shape
    ffn_specs = [
        _layer_spec(wg, layer),
        _layer_spec(wu, layer),
        _const_spec(cw.shape),
        _const_spec(cb.shape),
        _layer_spec(wd, layer),
    ]
    if mix is None:
        body, ins, specs = _ffn_kernel, (x,), [_tile_spec(D, ts)]
    else:
        ya, yb, wo = mix
        body, ins = _proj_ffn_kernel, (x, ya, yb, wo)
        specs = [_tile_spec(D, ts), _tile_spec(ya.shape[-1], ts), _lane_split_spec(yb.shape[1] * LANES, ts),
                 _const_spec(wo.shape)]
    return pl.pallas_call(
        functools.partial(body, ts=ts),
        grid=(B, S // ts),
        in_specs=specs + ffn_specs,
        out_specs=_tile_spec(D, ts),
        out_shape=jax.ShapeDtypeStruct((B, S, D), F32),
        scratch_shapes=[
            pltpu.VMEM((ts, D), BF16),
            pltpu.VMEM((ts, LANES), F32),
            pltpu.VMEM((D_FF // LANES, ts + SUBLANES, LANES), F32),
            pltpu.VMEM((ts, D_FF), F32),
            pltpu.VMEM((ts, D_FF), BF16),
        ],
        compiler_params=pltpu.CompilerParams(
            dimension_semantics=("arbitrary", "arbitrary"), vmem_limit_bytes=VMEM_LIMIT),
        name="conv_ffn" if mix is None else "proj_conv_ffn",
    )(*ins, wg, wu, cw, cb, wd)


def _odd_in_kernel(x_ref, g_ref, win_ref, cw_ref, cb_ref, wa_ref, ba_ref, wx_ref, bx_ref, lam_ref,
                   qg_ref, kg_ref, seg_ref,
                   ylru_ref, qkv0_ref, qkv1_ref, qkv2_ref,
                   h_buf, xin_buf, xc_buf, xcb_buf, ga_buf, gx_buf, gate_buf, t_buf, nq_buf, a_buf, b_buf,
                   al_buf, bl_buf, hin_buf, h_carry, *, ts):
    qkv_refs = (qkv0_ref, qkv1_ref, qkv2_ref)
    @pl.when(pl.program_id(1) == 0)
    def _():
        xin_buf[:, 0:SUBLANES, :] = jnp.zeros((LRU_WIDTH // LANES, SUBLANES, LANES), F32)
        h_carry[...] = jnp.zeros((SUBLANES, LRU_WIDTH), F32)

    all_cols = slice(None)
    h_buf[...] = _rms_bf16(x_ref[0], g_ref[...])

    def proj(c0, width):
        return jnp.dot(h_buf[...], win_ref[:, c0:c0 + width], preferred_element_type=F32)

    def lru_input():
        _store_lanes(xin_buf, SUBLANES, proj(0, LRU_WIDTH))
        for r in range(0, ts, ROW_CHUNK):
            acc = _bcast_rows(cb_ref, 0, all_cols, ROW_CHUNK)
            for k in range(LRU_TAPS):
                acc += (_bcast_rows(cw_ref, k, all_cols, ROW_CHUNK)
                        * _load_lanes(xin_buf, SUBLANES + r - (LRU_TAPS - 1) + k, ROW_CHUNK))
            xc_buf[r:r + ROW_CHUNK, :] = acc
            xcb_buf[r:r + ROW_CHUNK, :] = acc.astype(BF16)
        xin_buf[:, 0:SUBLANES, :] = xin_buf[:, ts:ts + SUBLANES, :]

    def lru_gates():
        for w_ref, buf in ((wa_ref, ga_buf), (wx_ref, gx_buf)):
            for c0 in range(0, LRU_WIDTH, GATE_BLOCK):
                cols = slice(c0, c0 + GATE_BLOCK)
                buf[:, cols] = jnp.dot(xcb_buf[:, cols], w_ref[cols, cols], preferred_element_type=F32)

    def lru_gate_proj():
        gate_buf[...] = proj(LRU_WIDTH, LRU_WIDTH)

    nlam = -lam_ref[...]
    softplus = jnp.maximum(nlam, 0.0) + jnp.log(1.0 + jnp.exp(-jnp.abs(nlam)))
    decay = jnp.broadcast_to(-LRU_C * softplus, (SUBLANES, LRU_WIDTH))
    row = lax.broadcasted_iota(jnp.int32, (SUBLANES, LRU_WIDTH), 0)

    def scan_group(g):
        rows = slice(g * SUBLANES, (g + 1) * SUBLANES)
        rg = _sigmoid(ga_buf[rows, :] + ba_ref[...])
        ig = _sigmoid(gx_buf[rows, :] + bx_ref[...])
        a = jnp.exp(decay * rg)
        b = jnp.sqrt(1.0 - a * a) * (ig * xc_buf[rows, :])
        for s in (1, 2, 4):
            a_prev = jnp.where(row >= s, pltpu.roll(a, s, 0), 1.0)
            b_prev = jnp.where(row >= s, pltpu.roll(b, s, 0), 0.0)
            b = a * b_prev + b
            a = a * a_prev
        a_buf[rows, :] = a
        b_buf[rows, :] = b
        al_buf[rows, :] = jnp.broadcast_to(a[SUBLANES - 1:SUBLANES, :], (SUBLANES, LRU_WIDTH))
        bl_buf[rows, :] = jnp.broadcast_to(b[SUBLANES - 1:SUBLANES, :], (SUBLANES, LRU_WIDTH))

    half = ts // 2
    q0 = 2 * LRU_WIDTH
    qk_rows = 4 * ROW_CHUNK
    norm_args = ((qg_ref, DIL_HEAD_DIM ** -0.5 * math.log2(math.e)), (kg_ref, 1.0))
    v_index = len(norm_args)
    n_split = DIL_OUT // LANES

    def qkv_proj(which, h):
        rows = slice(h * half, (h + 1) * half)
        c0 = q0 + which * DIL_QKV
        z = jnp.dot(h_buf[rows, :], win_ref[:, c0:c0 + DIL_QKV], preferred_element_type=F32)
        if which == v_index:
            for c in range(DIL_QKV // LANES):
                nq_buf[which, c, rows, :] = z[:, c * LANES:(c + 1) * LANES]
        else:
            t_buf[which, rows, :] = z

    def qk_norm(which, h):
        gain_ref, scale = norm_args[which]
        for r in range(h * half, (h + 1) * half, qk_rows):
            for g0 in range(0, DIL_QKV, DIL_OUT):
                t = t_buf[which, r:r + qk_rows, g0:g0 + DIL_OUT]
                ss = jnp.dot((t * t).astype(BF16), seg_ref[...], preferred_element_type=F32)
                gain = _bcast_rows(gain_ref, 0, slice(g0, g0 + DIL_OUT), qk_rows) * scale
                t = t * lax.rsqrt(ss * (1.0 / DIL_HEAD_DIM) + EPS) * gain
                for l in range(0, DIL_OUT, LANES):
                    nq_buf[which, (g0 + l) // LANES, r:r + qk_rows, :] = t[:, l:l + LANES]

    def scatter(which, h):
        for g, (_, d) in enumerate(DIL_PATTERNS):
            n = half // d
            for c in range(n_split):
                for r in range(d):
                    rows = slice(h * half, (h + 1) * half) if d == 1 else pl.ds(h * half + r, n, stride=d)
                    qkv_refs[g][0, which, c, r, h * n:(h + 1) * n, :] = (
                        nq_buf[which, g * n_split + c, rows, :].astype(BF16))

    lru_input()
    qkv_proj(0, 0)
    lru_gates()
    qkv_proj(1, 0)
    lru_gate_proj()
    qk_norm(0, 0)
    qkv_proj(v_index, 0)
    scatter(0, 0)
    qkv_proj(0, 1)
    qk_norm(1, 0)
    qkv_proj(1, 1)
    scatter(1, 0)
    scatter(v_index, 0)
    qkv_proj(v_index, 1)
    epilogues = {8: lambda: qk_norm(0, 1), 20: lambda: scatter(0, 1), 32: lambda: qk_norm(1, 1),
                 44: lambda: scatter(1, 1), 56: lambda: scatter(v_index, 1)}
    for g in range(ts // SUBLANES):
        scan_group(g)
        if g in epilogues:
            epilogues[g]()

    hc = h_carry[...]
    for r in range(0, ts, SUBLANES):
        rows = slice(r, r + SUBLANES)
        hin_buf[rows, :] = hc
        hc = al_buf[rows, :] * hc + bl_buf[rows, :]
    h_carry[...] = hc
    for r in range(0, ts, ROW_CHUNK):
        rows = slice(r, r + ROW_CHUNK)
        hs = a_buf[rows, :] * hin_buf[rows, :] + b_buf[rows, :]
        gate = gate_buf[rows, :]
        gelu = 0.5 * gate * (1.0 + jnp.tanh(math.sqrt(2.0 / math.pi) * (gate + 0.044715 * (gate * gate * gate))))
        ylru_ref[0, rows, :] = hs * gelu


def _odd_in(x, g, w_in, cw, cb, wa, ba, wx, bx, lam, qg, kg, seg, ts=SEQ_TILE):
    B, S, D = x.shape
    n_split = DIL_OUT // LANES
    return pl.pallas_call(
        functools.partial(_odd_in_kernel, ts=ts),
        grid=(B, S // ts),
        in_specs=[
            _tile_spec(D, ts),
            _const_spec((1, D)),
            _const_spec(w_in.shape),
            _const_spec(cw.shape),
            _const_spec(cb.shape),
            _const_spec(wa.shape),
            _const_spec(ba.shape),
            _const_spec(wx.shape),
            _const_spec(bx.shape),
            _const_spec((1, LRU_WIDTH)),
            _const_spec(qg.shape),
            _const_spec(kg.shape),
            _const_spec(seg.shape),
        ],
        out_specs=[_tile_spec(LRU_WIDTH, ts)] + [
            pl.BlockSpec((1, 3, n_split, d, ts // d, LANES), lambda b, j: (b, 0, 0, 0, j, 0))
            for _, d in DIL_PATTERNS],
        out_shape=[jax.ShapeDtypeStruct((B, S, LRU_WIDTH), F32)] + [
            jax.ShapeDtypeStruct((B, 3, n_split, d, S // d, LANES), BF16) for _, d in DIL_PATTERNS],
        scratch_shapes=[
            pltpu.VMEM((ts, D), BF16),
            pltpu.VMEM((LRU_WIDTH // LANES, ts + SUBLANES, LANES), F32),
            pltpu.VMEM((ts, LRU_WIDTH), F32),
            pltpu.VMEM((ts, LRU_WIDTH), BF16),
            pltpu.VMEM((ts, LRU_WIDTH), F32),
            pltpu.VMEM((ts, LRU_WIDTH), F32),
            pltpu.VMEM((ts, LRU_WIDTH), F32),
            pltpu.VMEM((2, ts, DIL_QKV), F32),
            pltpu.VMEM((3, DIL_QKV // LANES, ts, LANES), F32),
            pltpu.VMEM((ts, LRU_WIDTH), F32),
            pltpu.VMEM((ts, LRU_WIDTH), F32),
            pltpu.VMEM((ts, LRU_WIDTH), F32),
            pltpu.VMEM((ts, LRU_WIDTH), F32),
            pltpu.VMEM((ts, LRU_WIDTH), F32),
            pltpu.VMEM((SUBLANES, LRU_WIDTH), F32),
        ],
        compiler_params=pltpu.CompilerParams(
            dimension_semantics=("arbitrary", "arbitrary"), vmem_limit_bytes=VMEM_LIMIT),
        name="odd_in_lru",
    )(x, g, w_in, cw, cb, wa, ba, wx, bx, lam, qg, kg, seg)


def _attn_kernel(*refs, dilation, prev_dilation, last, seq):
    qkv_ref, refs = refs[0], refs[1:]
    if prev_dilation is not None:
        (prev_o_ref, prev_lse_ref), refs = refs[:2], refs[2:]
    if last:
        o_ref, bias_buf = refs
    else:
        o_ref, lse_ref, bias_buf = refs
    blk = ATTN_BLOCK
    nb = seq // dilation // blk
    n_split = DIL_OUT // LANES
    qi = lax.broadcasted_iota(jnp.int32, (DIL_HEADS * blk, 2 * blk), 0) % blk
    kj = lax.broadcasted_iota(jnp.int32, (DIL_HEADS * blk, 2 * blk), 1)
    for t in range(2):
        steps = qi - kj + t * blk
        bias_buf[t] = jnp.where((steps >= 0) & (steps <= blk), 0.0, -jnp.inf).astype(F32)

    lane_head = lax.broadcasted_iota(jnp.int32, (1, DIL_OUT), 1) // DIL_HEAD_DIM
    is_head = [lane_head == h for h in range(DIL_HEADS)]

    def fold(t):
        out = t[(DIL_HEADS - 1) * blk:DIL_HEADS * blk]
        for h in range(DIL_HEADS - 2, -1, -1):
            out = jnp.where(is_head[h], t[h * blk:(h + 1) * blk], out)
        return out

    def load(which, res, rows):
        return jnp.concatenate([qkv_ref[0, which, c, res, rows, :] for c in range(n_split)], axis=1)

    def body(idx, carry):
        res = idx // nb
        b = idx % nb
        kb = jnp.maximum(b - 1, 0)
        q_rows = pl.ds(pl.multiple_of(b * blk, blk), blk)
        k_rows = pl.ds(pl.multiple_of(kb * blk, blk), 2 * blk)
        qb = load(0, res, q_rows)
        zero = jnp.zeros_like(qb)
        qs = jnp.concatenate([jnp.where(is_head[h], qb, zero) for h in range(DIL_HEADS)], axis=0)
        s = lax.dot_general(qs, load(1, res, k_rows), (((1,), (1,)), ((), ())), preferred_element_type=F32)
        s = s + bias_buf[jnp.minimum(b, 1)]
        m = jnp.max(s, axis=-1, keepdims=True)
        p = jnp.exp2(s - m)
        l = jnp.sum(p, axis=-1, keepdims=True)
        o = jnp.dot(p.astype(BF16), load(2, res, k_rows), preferred_element_type=F32)
        o_tok, m_tok, l_tok = fold(o), fold(m), fold(l)
        o_tok = o_tok / l_tok
        lse = m_tok + jnp.log2(l_tok)
        if prev_dilation is not None:
            step = dilation // prev_dilation
            p_rows = pl.ds(b * (blk * step) + res // prev_dilation, blk, stride=step)
            p_res = res % prev_dilation
            prev_o = jnp.concatenate([prev_o_ref[0, c, p_res, p_rows, :] for c in range(n_split)], axis=1)
            prev_lse = jnp.concatenate([prev_lse_ref[0, c, p_res, p_rows, :] for c in range(n_split)], axis=1)
            top = jnp.maximum(prev_lse, lse)
            w_prev = jnp.exp2(prev_lse - top)
            w_new = jnp.exp2(lse - top)
            den = w_prev + w_new
            o_tok = (w_prev * prev_o + w_new * o_tok) / den
            lse = top + jnp.log2(den)
        if last:
            seq_rows = pl.ds(b * (blk * dilation) + res, blk, stride=dilation)
            for c in range(n_split):
                o_ref[0, c, seq_rows, :] = o_tok[:, c * LANES:(c + 1) * LANES]
        else:
            for c in range(n_split):
                o_ref[0, c, res, q_rows, :] = o_tok[:, c * LANES:(c + 1) * LANES]
                lse_ref[0, c, res, q_rows, :] = lse[:, c * LANES:(c + 1) * LANES]
        return carry

    lax.fori_loop(0, dilation * nb, body, 0, unroll=8)


def _dilated_attention(qkv_groups):
    def batch_block(shape):
        return pl.BlockSpec((1,) + shape[1:], lambda b: (b,) + (0,) * (len(shape) - 1))

    merged, prev_d = (), None
    for g, qkv in enumerate(qkv_groups):
        B, _, n_split, d, L, _ = qkv.shape
        assert DIL_PATTERNS[g] == (ATTN_BLOCK * d, d)
        last = g == len(qkv_groups) - 1
        if last:
            out_shape = [jax.ShapeDtypeStruct((B, n_split, d * L, LANES), F32)]
        else:
            out_shape = [jax.ShapeDtypeStruct((B, n_split, d, L, LANES), F32)] * 2
        ins = (qkv,) + tuple(merged)
        merged = pl.pallas_call(
            functools.partial(_attn_kernel, dilation=d, prev_dilation=prev_d, last=last, seq=d * L),
            grid=(B,),
            in_specs=[batch_block(a.shape) for a in ins],
            out_specs=[batch_block(s.shape) for s in out_shape],
            out_shape=out_shape,
            scratch_shapes=[pltpu.VMEM((2, DIL_HEADS * ATTN_BLOCK, 2 * ATTN_BLOCK), F32)],
            compiler_params=pltpu.CompilerParams(
                dimension_semantics=("arbitrary",), vmem_limit_bytes=VMEM_LIMIT),
            name=f"dilated_attention_d{d}",
        )(*ins)
        prev_d = d
    return merged[0]


def _block_diag(w):
    H, n, _ = w.shape
    eye = jnp.eye(H, dtype=w.dtype)
    return (eye[:, None, :, None] * w[:, :, None, :]).reshape(H * n, H * n)


def kernel(x, mix_norm_g, ffn_norm_g, ev_w_in, ev_sc_conv_w, ev_cf_conv_w, ev_cf_conv_b, ev_cf_ln_g, ev_cf_ln_b, ev_w_out, od_w_in, od_lru_conv_w, od_lru_conv_b, od_lru_wa, od_lru_ba, od_lru_wx, od_lru_bx, od_lru_lam, od_q_norm_g, od_k_norm_g, od_w_out, ffn_w_gate, ffn_w_up, ffn_conv_w, ffn_conv_b, ffn_w_down):
    row = lambda p: p.reshape(1, -1)
    bf = lambda w: w.astype(BF16)
    n_heads = DIL_QKV // DIL_HEAD_DIM
    head_of = jnp.arange(DIL_OUT) // DIL_HEAD_DIM
    seg = (head_of[:, None] == head_of[None, :]).astype(BF16)

    ffn_wg = bf(ffn_norm_g[:, :, None] * ffn_w_gate)
    ffn_wu = bf(ffn_norm_g[:, :, None] * ffn_w_up)
    ffn_wd = bf(ffn_w_down)

    def ffn(x, layer, mix=None):
        return _conv_ffn(x, layer, ffn_wg, ffn_wu,
                         _rep8(ffn_conv_w[layer]), _rep8(ffn_conv_b[layer]), ffn_wd, mix=mix)

    x = _even_mixer(x, row(mix_norm_g[0]), bf(ev_w_in[0]), _rep8(ev_sc_conv_w[0]), _rep8(ev_cf_conv_w[0]),
                    _rep8(ev_cf_conv_b[0]), _rep8(ev_cf_ln_g[0]), _rep8(ev_cf_ln_b[0]), bf(ev_w_out[0]))
    x = ffn(x, 0)
    y_lru, *qkv_groups = _odd_in(
        x, row(mix_norm_g[1]), bf(od_w_in[0]), _rep8(od_lru_conv_w[0]), _rep8(od_lru_conv_b[0]),
        bf(_block_diag(od_lru_wa[0])), _rep8(od_lru_ba[0]), bf(_block_diag(od_lru_wx[0])), _rep8(od_lru_bx[0]),
        row(od_lru_lam[0]), _rep8(jnp.tile(od_q_norm_g[0], n_heads)), _rep8(jnp.tile(od_k_norm_g[0], n_heads)), seg)
    y_att = _dilated_attention(qkv_groups)
    return ffn(x, 1, mix=(y_lru, y_att, bf(od_w_out[0])))
```

```python
import functools
import math

import jax
import jax.numpy as jnp
from jax import lax
from jax.experimental import pallas as pl
from jax.experimental.pallas import tpu as pltpu

F32 = jnp.float32
BF16 = jnp.bfloat16

D_MODEL = 1024
SC_WIDTH = 512
SC_TAPS = 3
CF_WIDTH = 512
CF_TAPS = 31
LRU_WIDTH = 512
LRU_TAPS = 4
LRU_C = 8.0
MXU_WIDTH = 256
GATE_BLOCK = MXU_WIDTH
DIL_PATTERNS = ((128, 1), (512, 4), (2048, 16))
DIL_HEADS = 4
DIL_HEAD_DIM = 64
DIL_OUT = DIL_HEADS * DIL_HEAD_DIM
DIL_QKV = len(DIL_PATTERNS) * DIL_OUT
ATTN_BLOCK = 128
D_FF = 2816
EPS = 1e-6

SUBLANES = 8
LANES = 128
SEQ_TILE = 512
CF_HALO = 32
ROW_CHUNK = 32
FFN_COL_CHUNK = 256
FFN_FIRST_PARTS = 2
VMEM_LIMIT = 56 * 1024 * 1024


def _rms_bf16(x, g):
    y = x * lax.rsqrt(jnp.mean(x * x, axis=-1, keepdims=True) + EPS)
    return (y * g).astype(BF16)


def _sigmoid(x):
    return 1.0 / (1.0 + jnp.exp(-x))


def _rep8(p):
    return jnp.repeat(p.reshape(-1, p.shape[-1]), SUBLANES, axis=0)


def _bcast_rows(ref, k, cols, n):
    w = ref[k * SUBLANES:(k + 1) * SUBLANES, cols]
    return jnp.concatenate([w] * (n // SUBLANES), axis=0)


def _load_lanes(ref, start, n):
    return jnp.concatenate([ref[c, start:start + n, :] for c in range(ref.shape[0])], axis=1)


def _lane_conv(ref, w_ref, n_taps, start, n):
    out = []
    for c in range(ref.shape[0]):
        cols = slice(c * LANES, (c + 1) * LANES)
        acc = None
        for k in range(n_taps):
            term = _bcast_rows(w_ref, k, cols, n) * ref[c, start + k:start + k + n, :]
            acc = term if acc is None else acc + term
        out.append(acc)
    return jnp.concatenate(out, axis=1)


def _store_lanes(ref, start, val):
    for c in range(ref.shape[0]):
        ref[c, start:start + val.shape[0], :] = val[:, c * LANES:(c + 1) * LANES]


def _const_spec(shape):
    return pl.BlockSpec(shape, lambda *_: (0,) * len(shape), pipeline_mode=pl.Buffered(1))


def _tile_spec(width, ts):
    return pl.BlockSpec((1, ts, width), lambda b, j: (b, j, 0))


def _lane_split_spec(width, ts):
    return pl.BlockSpec((1, width // LANES, ts, LANES), lambda b, j: (b, 0, j, 0))


def _even_kernel(x_ref, g_ref, win_ref, scw_ref, cfw_ref, cfb_ref, lng_ref, lnb_ref, wout_ref,
                 o_ref, h_buf, z_buf, cx_buf, u_buf, cat_buf, *, ts):
    @pl.when(pl.program_id(1) == 0)
    def _():
        cx_buf[:, 0:SUBLANES, :] = jnp.zeros((SC_WIDTH // LANES, SUBLANES, LANES), F32)
        u_buf[:, 0:CF_HALO, :] = jnp.zeros((CF_WIDTH // LANES, CF_HALO, LANES), F32)

    sc_b, sc_c, sc_x, cf_a, cf_g = range(5)
    all_cols = slice(None)
    half, quarter = ts // 2, ts // 4
    n_chunks = ts // ROW_CHUNK
    per_quarter = n_chunks // 4
    h_buf[...] = _rms_bf16(x_ref[0], g_ref[...])

    def proj_rows(c, h):
        rows = slice(h * half, (h + 1) * half)
        z_buf[c, rows, :] = jnp.dot(h_buf[rows, :], win_ref[:, c * SC_WIDTH:(c + 1) * SC_WIDTH],
                                    preferred_element_type=F32)

    def proj_cols(c, h):
        cols = slice(h * MXU_WIDTH, (h + 1) * MXU_WIDTH)
        z_buf[c, :, cols] = jnp.dot(h_buf[...], win_ref[:, c * SC_WIDTH + cols.start:c * SC_WIDTH + cols.stop],
                                    preferred_element_type=F32)

    def stage_u(k):
        rows = slice(k * ROW_CHUNK, (k + 1) * ROW_CHUNK)
        _store_lanes(u_buf, CF_HALO + rows.start, z_buf[cf_a, rows, :] * _sigmoid(z_buf[cf_g, rows, :]))

    def stage_cx(k):
        rows = slice(k * ROW_CHUNK, (k + 1) * ROW_CHUNK)
        _store_lanes(cx_buf, SUBLANES + rows.start, z_buf[sc_c, rows, :] * z_buf[sc_x, rows, :])

    def conv_sc(k):
        r = k * ROW_CHUNK
        acc = _lane_conv(cx_buf, scw_ref, SC_TAPS, SUBLANES + r - (SC_TAPS - 1), ROW_CHUNK)
        cat_buf[r:r + ROW_CHUNK, 0:SC_WIDTH] = (z_buf[sc_b, r:r + ROW_CHUNK, :] * acc).astype(BF16)

    def conv_cf(k):
        r = k * ROW_CHUNK
        u = _lane_conv(u_buf, cfw_ref, CF_TAPS, CF_HALO + r - (CF_TAPS - 1), ROW_CHUNK)
        u += _bcast_rows(cfb_ref, 0, all_cols, ROW_CHUNK)
        mu = jnp.mean(u, axis=-1, keepdims=True)
        uc = u - mu
        var = jnp.mean(uc * uc, axis=-1, keepdims=True)
        y = (uc * lax.rsqrt(var + EPS) * _bcast_rows(lng_ref, 0, all_cols, ROW_CHUNK)
             + _bcast_rows(lnb_ref, 0, all_cols, ROW_CHUNK))
        cat_buf[r:r + ROW_CHUNK, SC_WIDTH:SC_WIDTH + CF_WIDTH] = (y * _sigmoid(y)).astype(BF16)

    def out_proj(q):
        rows = slice(q * quarter, (q + 1) * quarter)
        for k in range(q * per_quarter, (q + 1) * per_quarter):
            conv_sc(k)
        o_ref[0, rows, :] = x_ref[0, rows, :] + jnp.dot(
            cat_buf[rows, :], wout_ref[...], preferred_element_type=F32)

    proj_rows(cf_a, 0)
    proj_rows(cf_g, 0)
    for k in range(n_chunks // 2):
        stage_u(k)
    after_chunk = {
        0: [lambda: proj_rows(cf_a, 1)],
        1: [lambda: proj_rows(cf_g, 1)] + [functools.partial(stage_u, k) for k in range(n_chunks // 2, n_chunks)],
        2: [lambda: proj_cols(sc_c, 0)], 3: [lambda: proj_cols(sc_c, 1)],
        4: [lambda: proj_cols(sc_x, 0)], 5: [lambda: proj_cols(sc_x, 1)],
        6: [lambda: proj_cols(sc_b, 0)], 7: [lambda: proj_cols(sc_b, 1)],
        8: [functools.partial(stage_cx, k) for k in range(n_chunks)],
        9: [lambda: out_proj(0)], 10: [lambda: out_proj(1)], 12: [lambda: out_proj(2)], 15: [lambda: out_proj(3)],
    }
    assert n_chunks == 16
    for k in range(n_chunks):
        conv_cf(k)
        for piece in after_chunk.get(k, ()):
            piece()

    cx_buf[:, 0:SUBLANES, :] = cx_buf[:, ts:ts + SUBLANES, :]
    u_buf[:, 0:CF_HALO, :] = u_buf[:, ts:ts + CF_HALO, :]


def _even_mixer(x, g, w_in, sc_w, cf_w, cf_b, ln_g, ln_b, w_out, ts=SEQ_TILE):
    B, S, D = x.shape
    ev_in = w_in.shape[1]
    return pl.pallas_call(
        functools.partial(_even_kernel, ts=ts),
        grid=(B, S // ts),
        in_specs=[
            _tile_spec(D, ts),
            _const_spec((1, D)),
            _const_spec((D, ev_in)),
            _const_spec(sc_w.shape),
            _const_spec(cf_w.shape),
            _const_spec(cf_b.shape),
            _const_spec(ln_g.shape),
            _const_spec(ln_b.shape),
            _const_spec(w_out.shape),
        ],
        out_specs=_tile_spec(D, ts),
        out_shape=jax.ShapeDtypeStruct((B, S, D), F32),
        scratch_shapes=[
            pltpu.VMEM((ts, D), BF16),
            pltpu.VMEM((ev_in // SC_WIDTH, ts, SC_WIDTH), F32),
            pltpu.VMEM((SC_WIDTH // LANES, ts + SUBLANES, LANES), F32),
            pltpu.VMEM((CF_WIDTH // LANES, ts + CF_HALO, LANES), F32),
            pltpu.VMEM((ts, SC_WIDTH + CF_WIDTH), BF16),
        ],
        compiler_params=pltpu.CompilerParams(
            dimension_semantics=("arbitrary", "arbitrary"), vmem_limit_bytes=VMEM_LIMIT),
        name="even_mixer",
    )(x, g, w_in, sc_w, cf_w, cf_b, ln_g, ln_b, w_out)


def _ffn_body(x, ng_ref, wg_ref, wu_ref, cw_ref, cb_ref, wd_ref, o_ref, h_buf, g_buf, up_buf, act_buf, ts):
    @pl.when(pl.program_id(1) == 0)
    def _():
        g_buf[:, 0:SUBLANES, :] = jnp.zeros((D_FF // LANES, SUBLANES, LANES), F32)

    def g_rows(c0, start):
        return jnp.concatenate([g_buf[(c0 + l) // LANES, start:start + ROW_CHUNK, :]
                                for l in range(0, FFN_COL_CHUNK, LANES)], axis=1)

    first_parts = [slice(r, r + ts // FFN_FIRST_PARTS) for r in range(0, ts, ts // FFN_FIRST_PARTS)]
    for rows in first_parts:
        h_buf[rows, :] = _rms_bf16(x[rows], ng_ref[...])
    for c0 in range(0, D_FF, FFN_COL_CHUNK):
        cols = slice(c0, c0 + FFN_COL_CHUNK)
        for rows in (first_parts if c0 == 0 else [slice(0, ts)]):
            g = jnp.dot(h_buf[rows, :], wg_ref[:, cols], preferred_element_type=F32)
            for l in range(0, FFN_COL_CHUNK, LANES):
                g_buf[(c0 + l) // LANES, SUBLANES + rows.start:SUBLANES + rows.stop, :] = g[:, l:l + LANES]
            up_buf[rows, cols] = jnp.dot(h_buf[rows, :], wu_ref[:, cols], preferred_element_type=F32)
        w0, w1, w2 = (_bcast_rows(cw_ref, k, cols, ROW_CHUNK) for k in range(3))
        bias = _bcast_rows(cb_ref, 0, cols, ROW_CHUNK)
        for r in range(0, ts, ROW_CHUNK):
            gc = (w0 * g_rows(c0, r + SUBLANES - 2) + w1 * g_rows(c0, r + SUBLANES - 1)
                  + w2 * g_rows(c0, r + SUBLANES) + bias)
            act_buf[r:r + ROW_CHUNK, cols] = (gc * _sigmoid(gc) * up_buf[r:r + ROW_CHUNK, cols]).astype(BF16)
    o_ref[0] = x + jnp.dot(act_buf[...], wd_ref[...], preferred_element_type=F32)
    g_buf[:, 0:SUBLANES, :] = g_buf[:, ts:ts + SUBLANES, :]


def _ffn_kernel(x_ref, ng_ref, wg_ref, wu_ref, cw_ref, cb_ref, wd_ref, o_ref,
                *scratch, ts):
    _ffn_body(x_ref[0], ng_ref, wg_ref, wu_ref, cw_ref, cb_ref, wd_ref, o_ref, *scratch, ts)


def _proj_ffn_kernel(x_ref, ya_ref, yb_ref, wo_ref, ng_ref, wg_ref, wu_ref, cw_ref, cb_ref, wd_ref, o_ref,
                     *scratch, ts):
    wa = ya_ref.shape[-1]
    yb = jnp.concatenate([yb_ref[0, c] for c in range(yb_ref.shape[1])], axis=1)
    x = x_ref[0] + jnp.dot(ya_ref[0].astype(BF16), wo_ref[0:wa, :], preferred_element_type=F32)
    x = x + jnp.dot(yb.astype(BF16), wo_ref[wa:, :], preferred_element_type=F32)
    _ffn_body(x, ng_ref, wg_ref, wu_ref, cw_ref, cb_ref, wd_ref, o_ref, *scratch, ts)


def _layer_spec(stacked, layer):
    zeros = (0,) * (stacked.ndim - 1)
    return pl.BlockSpec((None,) + stacked.shape[1:], lambda *_: (layer,) + zeros, pipeline_mode=pl.Buffered(1))


def _conv_ffn(x, layer, ng, wg, wu, cw, cb, wd, mix=None, ts=SEQ_TILE):
    B, S, D = x.shape
    ffn_specs = [
        _const_spec((1, D)),
        _layer_spec(wg, layer),
        _layer_spec(wu, layer),
        _const_spec(cw.shape),
        _const_spec(cb.shape),
        _layer_spec(wd, layer),
    ]
    if mix is None:
        body, ins, specs = _ffn_kernel, (x,), [_tile_spec(D, ts)]
    else:
        ya, yb, wo = mix
        body, ins = _proj_ffn_kernel, (x, ya, yb, wo)
        specs = [_tile_spec(D, ts), _tile_spec(ya.shape[-1], ts), _lane_split_spec(yb.shape[1] * LANES, ts),
                 _const_spec(wo.shape)]
    return pl.pallas_call(
        functools.partial(body, ts=ts),
        grid=(B, S // ts),
        in_specs=specs + ffn_specs,
        out_specs=_tile_spec(D, ts),
        out_shape=jax.ShapeDtypeStruct((B, S, D), F32),
        scratch_shapes=[
            pltpu.VMEM((ts, D), BF16),
            pltpu.VMEM((D_FF // LANES, ts + SUBLANES, LANES), F32),
            pltpu.VMEM((ts, D_FF), F32),
            pltpu.VMEM((ts, D_FF), BF16),
        ],
        compiler_params=pltpu.CompilerParams(
            dimension_semantics=("arbitrary", "arbitrary"), vmem_limit_bytes=VMEM_LIMIT),
        name="conv_ffn" if mix is None else "proj_conv_ffn",
    )(*ins, ng, wg, wu, cw, cb, wd)


def _odd_in_kernel(x_ref, g_ref, win_ref, cw_ref, cb_ref, wa_ref, ba_ref, wx_ref, bx_ref, lam_ref,
                   qg_ref, kg_ref, seg_ref,
                   ylru_ref, qkv0_ref, qkv1_ref, qkv2_ref,
                   h_buf, xin_buf, xc_buf, xcb_buf, ga_buf, gx_buf, gate_buf, t_buf, nq_buf, a_buf, b_buf,
                   al_buf, bl_buf, hin_buf, h_carry, *, ts):
    qkv_refs = (qkv0_ref, qkv1_ref, qkv2_ref)
    @pl.when(pl.program_id(1) == 0)
    def _():
        xin_buf[:, 0:SUBLANES, :] = jnp.zeros((LRU_WIDTH // LANES, SUBLANES, LANES), F32)
        h_carry[...] = jnp.zeros((SUBLANES, LRU_WIDTH), F32)

    all_cols = slice(None)
    h_buf[...] = _rms_bf16(x_ref[0], g_ref[...])

    def proj(c0, width):
        return jnp.dot(h_buf[...], win_ref[:, c0:c0 + width], preferred_element_type=F32)

    def lru_input():
        _store_lanes(xin_buf, SUBLANES, proj(0, LRU_WIDTH))
        for r in range(0, ts, ROW_CHUNK):
            acc = _bcast_rows(cb_ref, 0, all_cols, ROW_CHUNK)
            for k in range(LRU_TAPS):
                acc += (_bcast_rows(cw_ref, k, all_cols, ROW_CHUNK)
                        * _load_lanes(xin_buf, SUBLANES + r - (LRU_TAPS - 1) + k, ROW_CHUNK))
            xc_buf[r:r + ROW_CHUNK, :] = acc
            xcb_buf[r:r + ROW_CHUNK, :] = acc.astype(BF16)
        xin_buf[:, 0:SUBLANES, :] = xin_buf[:, ts:ts + SUBLANES, :]

    def lru_gates():
        for w_ref, buf in ((wa_ref, ga_buf), (wx_ref, gx_buf)):
            for c0 in range(0, LRU_WIDTH, GATE_BLOCK):
                cols = slice(c0, c0 + GATE_BLOCK)
                buf[:, cols] = jnp.dot(xcb_buf[:, cols], w_ref[cols, cols], preferred_element_type=F32)

    def lru_gate_proj():
        gate_buf[...] = proj(LRU_WIDTH, LRU_WIDTH)

    nlam = -lam_ref[...]
    softplus = jnp.maximum(nlam, 0.0) + jnp.log(1.0 + jnp.exp(-jnp.abs(nlam)))
    decay = jnp.broadcast_to(-LRU_C * softplus, (SUBLANES, LRU_WIDTH))
    row = lax.broadcasted_iota(jnp.int32, (SUBLANES, LRU_WIDTH), 0)

    def scan_group(g):
        rows = slice(g * SUBLANES, (g + 1) * SUBLANES)
        rg = _sigmoid(ga_buf[rows, :] + ba_ref[...])
        ig = _sigmoid(gx_buf[rows, :] + bx_ref[...])
        a = jnp.exp(decay * rg)
        b = jnp.sqrt(1.0 - a * a) * (ig * xc_buf[rows, :])
        for s in (1, 2, 4):
            a_prev = jnp.where(row >= s, pltpu.roll(a, s, 0), 1.0)
            b_prev = jnp.where(row >= s, pltpu.roll(b, s, 0), 0.0)
            b = a * b_prev + b
            a = a * a_prev
        a_buf[rows, :] = a
        b_buf[rows, :] = b
        al_buf[rows, :] = jnp.broadcast_to(a[SUBLANES - 1:SUBLANES, :], (SUBLANES, LRU_WIDTH))
        bl_buf[rows, :] = jnp.broadcast_to(b[SUBLANES - 1:SUBLANES, :], (SUBLANES, LRU_WIDTH))

    half = ts // 2
    q0 = 2 * LRU_WIDTH
    qk_rows = 4 * ROW_CHUNK
    norm_args = ((qg_ref, DIL_HEAD_DIM ** -0.5 * math.log2(math.e)), (kg_ref, 1.0))
    v_index = len(norm_args)
    n_split = DIL_OUT // LANES

    def qkv_proj(which, h):
        rows = slice(h * half, (h + 1) * half)
        c0 = q0 + which * DIL_QKV
        z = jnp.dot(h_buf[rows, :], win_ref[:, c0:c0 + DIL_QKV], preferred_element_type=F32)
        if which == v_index:
            for c in range(DIL_QKV // LANES):
                nq_buf[which, c, rows, :] = z[:, c * LANES:(c + 1) * LANES]
        else:
            t_buf[which, rows, :] = z

    def qk_norm(which, h):
        gain_ref, scale = norm_args[which]
        for r in range(h * half, (h + 1) * half, qk_rows):
            for g0 in range(0, DIL_QKV, DIL_OUT):
                t = t_buf[which, r:r + qk_rows, g0:g0 + DIL_OUT]
                ss = jnp.dot((t * t).astype(BF16), seg_ref[...], preferred_element_type=F32)
                gain = _bcast_rows(gain_ref, 0, slice(g0, g0 + DIL_OUT), qk_rows) * scale
                t = t * lax.rsqrt(ss * (1.0 / DIL_HEAD_DIM) + EPS) * gain
                for l in range(0, DIL_OUT, LANES):
                    nq_buf[which, (g0 + l) // LANES, r:r + qk_rows, :] = t[:, l:l + LANES]

    def scatter(which, h):
        for g, (_, d) in enumerate(DIL_PATTERNS):
            n = half // d
            for c in range(n_split):
                for r in range(d):
                    rows = slice(h * half, (h + 1) * half) if d == 1 else pl.ds(h * half + r, n, stride=d)
                    qkv_refs[g][0, which, c, r, h * n:(h + 1) * n, :] = (
                        nq_buf[which, g * n_split + c, rows, :].astype(BF16))

    lru_input()
    qkv_proj(0, 0)
    lru_gates()
    qkv_proj(1, 0)
    lru_gate_proj()
    qk_norm(0, 0)
    qkv_proj(v_index, 0)
    scatter(0, 0)
    qkv_proj(0, 1)
    qk_norm(1, 0)
    qkv_proj(1, 1)
    scatter(1, 0)
    scatter(v_index, 0)
    qkv_proj(v_index, 1)
    epilogues = {8: lambda: qk_norm(0, 1), 20: lambda: scatter(0, 1), 32: lambda: qk_norm(1, 1),
                 44: lambda: scatter(1, 1), 56: lambda: scatter(v_index, 1)}
    for g in range(ts // SUBLANES):
        scan_group(g)
        if g in epilogues:
            epilogues[g]()

    hc = h_carry[...]
    for r in range(0, ts, SUBLANES):
        rows = slice(r, r + SUBLANES)
        hin_buf[rows, :] = hc
        hc = al_buf[rows, :] * hc + bl_buf[rows, :]
    h_carry[...] = hc
    for r in range(0, ts, ROW_CHUNK):
        rows = slice(r, r + ROW_CHUNK)
        hs = a_buf[rows, :] * hin_buf[rows, :] + b_buf[rows, :]
        gate = gate_buf[rows, :]
        gelu = 0.5 * gate * (1.0 + jnp.tanh(math.sqrt(2.0 / math.pi) * (gate + 0.044715 * (gate * gate * gate))))
        ylru_ref[0, rows, :] = hs * gelu


def _odd_in(x, g, w_in, cw, cb, wa, ba, wx, bx, lam, qg, kg, seg, ts=SEQ_TILE):
    B, S, D = x.shape
    n_split = DIL_OUT // LANES
    return pl.pallas_call(
        functools.partial(_odd_in_kernel, ts=ts),
        grid=(B, S // ts),
        in_specs=[
            _tile_spec(D, ts),
            _const_spec((1, D)),
            _const_spec(w_in.shape),
            _const_spec(cw.shape),
            _const_spec(cb.shape),
            _const_spec(wa.shape),
            _const_spec(ba.shape),
            _const_spec(wx.shape),
            _const_spec(bx.shape),
            _const_spec((1, LRU_WIDTH)),
            _const_spec(qg.shape),
            _const_spec(kg.shape),
            _const_spec(seg.shape),
        ],
        out_specs=[_tile_spec(LRU_WIDTH, ts)] + [
            pl.BlockSpec((1, 3, n_split, d, ts // d, LANES), lambda b, j: (b, 0, 0, 0, j, 0))
            for _, d in DIL_PATTERNS],
        out_shape=[jax.ShapeDtypeStruct((B, S, LRU_WIDTH), F32)] + [
            jax.ShapeDtypeStruct((B, 3, n_split, d, S // d, LANES), BF16) for _, d in DIL_PATTERNS],
        scratch_shapes=[
            pltpu.VMEM((ts, D), BF16),
            pltpu.VMEM((LRU_WIDTH // LANES, ts + SUBLANES, LANES), F32),
            pltpu.VMEM((ts, LRU_WIDTH), F32),
            pltpu.VMEM((ts, LRU_WIDTH), BF16),
            pltpu.VMEM((ts, LRU_WIDTH), F32),
            pltpu.VMEM((ts, LRU_WIDTH), F32),
            pltpu.VMEM((ts, LRU_WIDTH), F32),
            pltpu.VMEM((2, ts, DIL_QKV), F32),
            pltpu.VMEM((3, DIL_QKV // LANES, ts, LANES), F32),
            pltpu.VMEM((ts, LRU_WIDTH), F32),
            pltpu.VMEM((ts, LRU_WIDTH), F32),
            pltpu.VMEM((ts, LRU_WIDTH), F32),
            pltpu.VMEM((ts, LRU_WIDTH), F32),
            pltpu.VMEM((ts, LRU_WIDTH), F32),
            pltpu.VMEM((SUBLANES, LRU_WIDTH), F32),
        ],
        compiler_params=pltpu.CompilerParams(
            dimension_semantics=("arbitrary", "arbitrary"), vmem_limit_bytes=VMEM_LIMIT),
        name="odd_in_lru",
    )(x, g, w_in, cw, cb, wa, ba, wx, bx, lam, qg, kg, seg)


def _attn_kernel(*refs, dilation, prev_dilation, last, seq):
    qkv_ref, refs = refs[0], refs[1:]
    if prev_dilation is not None:
        (prev_o_ref, prev_lse_ref), refs = refs[:2], refs[2:]
    if last:
        o_ref, bias_buf = refs
    else:
        o_ref, lse_ref, bias_buf = refs
    blk = ATTN_BLOCK
    nb = seq // dilation // blk
    n_split = DIL_OUT // LANES
    qi = lax.broadcasted_iota(jnp.int32, (DIL_HEADS * blk, 2 * blk), 0) % blk
    kj = lax.broadcasted_iota(jnp.int32, (DIL_HEADS * blk, 2 * blk), 1)
    for t in range(2):
        steps = qi - kj + t * blk
        bias_buf[t] = jnp.where((steps >= 0) & (steps <= blk), 0.0, -jnp.inf).astype(F32)

    lane_head = lax.broadcasted_iota(jnp.int32, (1, DIL_OUT), 1) // DIL_HEAD_DIM
    is_head = [lane_head == h for h in range(DIL_HEADS)]

    def fold(t):
        out = t[(DIL_HEADS - 1) * blk:DIL_HEADS * blk]
        for h in range(DIL_HEADS - 2, -1, -1):
            out = jnp.where(is_head[h], t[h * blk:(h + 1) * blk], out)
        return out

    def load(which, res, rows):
        return jnp.concatenate([qkv_ref[0, which, c, res, rows, :] for c in range(n_split)], axis=1)

    def body(idx, carry):
        res = idx // nb
        b = idx % nb
        kb = jnp.maximum(b - 1, 0)
        q_rows = pl.ds(pl.multiple_of(b * blk, blk), blk)
        k_rows = pl.ds(pl.multiple_of(kb * blk, blk), 2 * blk)
        qb = load(0, res, q_rows)
        zero = jnp.zeros_like(qb)
        qs = jnp.concatenate([jnp.where(is_head[h], qb, zero) for h in range(DIL_HEADS)], axis=0)
        s = lax.dot_general(qs, load(1, res, k_rows), (((1,), (1,)), ((), ())), preferred_element_type=F32)
        s = s + bias_buf[jnp.minimum(b, 1)]
        m = jnp.max(s, axis=-1, keepdims=True)
        p = jnp.exp2(s - m)
        l = jnp.sum(p, axis=-1, keepdims=True)
        o = jnp.dot(p.astype(BF16), load(2, res, k_rows), preferred_element_type=F32)
        o_tok, m_tok, l_tok = fold(o), fold(m), fold(l)
        o_tok = o_tok / l_tok
        lse = m_tok + jnp.log2(l_tok)
        if prev_dilation is not None:
            step = dilation // prev_dilation
            p_rows = pl.ds(b * (blk * step) + res // prev_dilation, blk, stride=step)
            p_res = res % prev_dilation
            prev_o = jnp.concatenate([prev_o_ref[0, c, p_res, p_rows, :] for c in range(n_split)], axis=1)
            prev_lse = jnp.concatenate([prev_lse_ref[0, c, p_res, p_rows, :] for c in range(n_split)], axis=1)
            top = jnp.maximum(prev_lse, lse)
            w_prev = jnp.exp2(prev_lse - top)
            w_new = jnp.exp2(lse - top)
            den = w_prev + w_new
            o_tok = (w_prev * prev_o + w_new * o_tok) / den
            lse = top + jnp.log2(den)
        if last:
            seq_rows = pl.ds(b * (blk * dilation) + res, blk, stride=dilation)
            for c in range(n_split):
                o_ref[0, c, seq_rows, :] = o_tok[:, c * LANES:(c + 1) * LANES]
        else:
            for c in range(n_split):
                o_ref[0, c, res, q_rows, :] = o_tok[:, c * LANES:(c + 1) * LANES]
                lse_ref[0, c, res, q_rows, :] = lse[:, c * LANES:(c + 1) * LANES]
        return carry

    lax.fori_loop(0, dilation * nb, body, 0, unroll=8)


def _dilated_attention(qkv_groups):
    def batch_block(shape):
        return pl.BlockSpec((1,) + shape[1:], lambda b: (b,) + (0,) * (len(shape) - 1))

    merged, prev_d = (), None
    for g, qkv in enumerate(qkv_groups):
        B, _, n_split, d, L, _ = qkv.shape
        assert DIL_PATTERNS[g] == (ATTN_BLOCK * d, d)
        last = g == len(qkv_groups) - 1
        if last:
            out_shape = [jax.ShapeDtypeStruct((B, n_split, d * L, LANES), F32)]
        else:
            out_shape = [jax.ShapeDtypeStruct((B, n_split, d, L, LANES), F32)] * 2
        ins = (qkv,) + tuple(merged)
        merged = pl.pallas_call(
            functools.partial(_attn_kernel, dilation=d, prev_dilation=prev_d, last=last, seq=d * L),
            grid=(B,),
            in_specs=[batch_block(a.shape) for a in ins],
            out_specs=[batch_block(s.shape) for s in out_shape],
            out_shape=out_shape,
            scratch_shapes=[pltpu.VMEM((2, DIL_HEADS * ATTN_BLOCK, 2 * ATTN_BLOCK), F32)],
            compiler_params=pltpu.CompilerParams(
                dimension_semantics=("arbitrary",), vmem_limit_bytes=VMEM_LIMIT),
            name=f"dilated_attention_d{d}",
        )(*ins)
        prev_d = d
    return merged[0]


def _block_diag(w):
    H, n, _ = w.shape
    eye = jnp.eye(H, dtype=w.dtype)
    return (eye[:, None, :, None] * w[:, :, None, :]).reshape(H * n, H * n)


def kernel(x, mix_norm_g, ffn_norm_g, ev_w_in, ev_sc_conv_w, ev_cf_conv_w, ev_cf_conv_b, ev_cf_ln_g, ev_cf_ln_b, ev_w_out, od_w_in, od_lru_conv_w, od_lru_conv_b, od_lru_wa, od_lru_ba, od_lru_wx, od_lru_bx, od_lru_lam, od_q_norm_g, od_k_norm_g, od_w_out, ffn_w_gate, ffn_w_up, ffn_conv_w, ffn_conv_b, ffn_w_down):
    row = lambda p: p.reshape(1, -1)
    bf = lambda w: w.astype(BF16)
    n_heads = DIL_QKV // DIL_HEAD_DIM
    head_of = jnp.arange(DIL_OUT) // DIL_HEAD_DIM
    seg = (head_of[:, None] == head_of[None, :]).astype(BF16)

    ffn_wg, ffn_wu, ffn_wd = bf(ffn_w_gate), bf(ffn_w_up), bf(ffn_w_down)

    def ffn(x, layer, mix=None):
        return _conv_ffn(x, layer, row(ffn_norm_g[layer]), ffn_wg, ffn_wu,
                         _rep8(ffn_conv_w[layer]), _rep8(ffn_conv_b[layer]), ffn_wd, mix=mix)

    x = _even_mixer(x, row(mix_norm_g[0]), bf(ev_w_in[0]), _rep8(ev_sc_conv_w[0]), _rep8(ev_cf_conv_w[0]),
                    _rep8(ev_cf_conv_b[0]), _rep8(ev_cf_ln_g[0]), _rep8(ev_cf_ln_b[0]), bf(ev_w_out[0]))
    x = ffn(x, 0)
    y_lru, *qkv_groups = _odd_in(
        x, row(mix_norm_g[1]), bf(od_w_in[0]), _rep8(od_lru_conv_w[0]), _rep8(od_lru_conv_b[0]),
        bf(_block_diag(od_lru_wa[0])), _rep8(od_lru_ba[0]), bf(_block_diag(od_lru_wx[0])), _rep8(od_lru_bx[0]),
        row(od_lru_lam[0]), _rep8(jnp.tile(od_q_norm_g[0], n_heads)), _rep8(jnp.tile(od_k_norm_g[0], n_heads)), seg)
    y_att = _dilated_attention(qkv_groups)
    return ffn(x, 1, mix=(y_lru, y_att, bf(od_w_out[0])))
```

```python
import functools
import math

import jax
import jax.numpy as jnp
from jax import lax
from jax.experimental import pallas as pl
from jax.experimental.pallas import tpu as pltpu

F32 = jnp.float32
BF16 = jnp.bfloat16

D_MODEL = 1024
SC_WIDTH = 512
SC_TAPS = 3
CF_WIDTH = 512
CF_TAPS = 31
LRU_WIDTH = 512
LRU_TAPS = 4
LRU_C = 8.0
MXU_WIDTH = 256
GATE_BLOCK = MXU_WIDTH
DIL_PATTERNS = ((128, 1), (512, 4), (2048, 16))
DIL_HEADS = 4
DIL_HEAD_DIM = 64
DIL_OUT = DIL_HEADS * DIL_HEAD_DIM
DIL_QKV = len(DIL_PATTERNS) * DIL_OUT
ATTN_BLOCK = 128
D_FF = 2816
EPS = 1e-6

SUBLANES = 8
LANES = 128
SEQ_TILE = 512
CF_HALO = 32
ROW_CHUNK = 32
FFN_COL_CHUNK = 256
W_STAGE_ROWS = 256
VMEM_LIMIT = 56 * 1024 * 1024


def _rms_bf16(x, g):
    y = x * lax.rsqrt(jnp.mean(x * x, axis=-1, keepdims=True) + EPS)
    return (y * g).astype(BF16)


def _sigmoid(x):
    return 1.0 / (1.0 + jnp.exp(-x))


def _rep8(p):
    return jnp.repeat(p.reshape(-1, p.shape[-1]), SUBLANES, axis=0)


def _bcast_rows(ref, k, cols, n):
    w = ref[k * SUBLANES:(k + 1) * SUBLANES, cols]
    return jnp.concatenate([w] * (n // SUBLANES), axis=0)


def _load_lanes(ref, start, n):
    return jnp.concatenate([ref[c, start:start + n, :] for c in range(ref.shape[0])], axis=1)


def _lane_conv(ref, w_ref, n_taps, start, n):
    out = []
    for c in range(ref.shape[0]):
        cols = slice(c * LANES, (c + 1) * LANES)
        acc = None
        for k in range(n_taps):
            term = _bcast_rows(w_ref, k, cols, n) * ref[c, start + k:start + k + n, :]
            acc = term if acc is None else acc + term
        out.append(acc)
    return jnp.concatenate(out, axis=1)


def _store_lanes(ref, start, val):
    for c in range(ref.shape[0]):
        ref[c, start:start + val.shape[0], :] = val[:, c * LANES:(c + 1) * LANES]


def _const_spec(shape):
    return pl.BlockSpec(shape, lambda *_: (0,) * len(shape), pipeline_mode=pl.Buffered(1))


def _tile_spec(width, ts):
    return pl.BlockSpec((1, ts, width), lambda b, j: (b, j, 0))


def _lane_split_spec(width, ts):
    return pl.BlockSpec((1, width // LANES, ts, LANES), lambda b, j: (b, 0, j, 0))


def _even_kernel(x_ref, g_ref, win_ref, scw_ref, cfw_ref, cfb_ref, lng_ref, lnb_ref, wout_ref,
                 o_ref, h_buf, z_buf, cx_buf, u_buf, cat_buf, *, ts):
    @pl.when(pl.program_id(1) == 0)
    def _():
        cx_buf[:, 0:SUBLANES, :] = jnp.zeros((SC_WIDTH // LANES, SUBLANES, LANES), F32)
        u_buf[:, 0:CF_HALO, :] = jnp.zeros((CF_WIDTH // LANES, CF_HALO, LANES), F32)

    sc_b, sc_c, sc_x, cf_a, cf_g = range(5)
    all_cols = slice(None)
    half, quarter = ts // 2, ts // 4
    n_chunks = ts // ROW_CHUNK
    per_quarter = n_chunks // 4
    h_buf[...] = _rms_bf16(x_ref[0], g_ref[...])

    def proj_rows(c, h):
        rows = slice(h * half, (h + 1) * half)
        z_buf[c, rows, :] = jnp.dot(h_buf[rows, :], win_ref[:, c * SC_WIDTH:(c + 1) * SC_WIDTH],
                                    preferred_element_type=F32)

    def proj_cols(c, h):
        cols = slice(h * MXU_WIDTH, (h + 1) * MXU_WIDTH)
        z_buf[c, :, cols] = jnp.dot(h_buf[...], win_ref[:, c * SC_WIDTH + cols.start:c * SC_WIDTH + cols.stop],
                                    preferred_element_type=F32)

    def stage_u(k):
        rows = slice(k * ROW_CHUNK, (k + 1) * ROW_CHUNK)
        _store_lanes(u_buf, CF_HALO + rows.start, z_buf[cf_a, rows, :] * _sigmoid(z_buf[cf_g, rows, :]))

    def stage_cx(k):
        rows = slice(k * ROW_CHUNK, (k + 1) * ROW_CHUNK)
        _store_lanes(cx_buf, SUBLANES + rows.start, z_buf[sc_c, rows, :] * z_buf[sc_x, rows, :])

    def conv_sc(k):
        r = k * ROW_CHUNK
        acc = _lane_conv(cx_buf, scw_ref, SC_TAPS, SUBLANES + r - (SC_TAPS - 1), ROW_CHUNK)
        cat_buf[r:r + ROW_CHUNK, 0:SC_WIDTH] = (z_buf[sc_b, r:r + ROW_CHUNK, :] * acc).astype(BF16)

    def conv_cf(k):
        r = k * ROW_CHUNK
        u = _lane_conv(u_buf, cfw_ref, CF_TAPS, CF_HALO + r - (CF_TAPS - 1), ROW_CHUNK)
        u += _bcast_rows(cfb_ref, 0, all_cols, ROW_CHUNK)
        mu = jnp.mean(u, axis=-1, keepdims=True)
        uc = u - mu
        var = jnp.mean(uc * uc, axis=-1, keepdims=True)
        y = (uc * lax.rsqrt(var + EPS) * _bcast_rows(lng_ref, 0, all_cols, ROW_CHUNK)
             + _bcast_rows(lnb_ref, 0, all_cols, ROW_CHUNK))
        cat_buf[r:r + ROW_CHUNK, SC_WIDTH:SC_WIDTH + CF_WIDTH] = (y * _sigmoid(y)).astype(BF16)

    def out_proj(q):
        rows = slice(q * quarter, (q + 1) * quarter)
        for k in range(q * per_quarter, (q + 1) * per_quarter):
            conv_sc(k)
        o_ref[0, rows, :] = x_ref[0, rows, :] + jnp.dot(
            cat_buf[rows, :], wout_ref[...], preferred_element_type=F32)

    proj_rows(cf_a, 0)
    proj_rows(cf_g, 0)
    for k in range(n_chunks // 2):
        stage_u(k)
    after_chunk = {
        0: [lambda: proj_rows(cf_a, 1)],
        1: [lambda: proj_rows(cf_g, 1)] + [functools.partial(stage_u, k) for k in range(n_chunks // 2, n_chunks)],
        2: [lambda: proj_cols(sc_c, 0)], 3: [lambda: proj_cols(sc_c, 1)],
        4: [lambda: proj_cols(sc_x, 0)], 5: [lambda: proj_cols(sc_x, 1)],
        6: [lambda: proj_cols(sc_b, 0)], 7: [lambda: proj_cols(sc_b, 1)],
        8: [functools.partial(stage_cx, k) for k in range(n_chunks)],
        9: [lambda: out_proj(0)], 10: [lambda: out_proj(1)], 12: [lambda: out_proj(2)], 15: [lambda: out_proj(3)],
    }
    assert n_chunks == 16
    for k in range(n_chunks):
        conv_cf(k)
        for piece in after_chunk.get(k, ()):
            piece()

    cx_buf[:, 0:SUBLANES, :] = cx_buf[:, ts:ts + SUBLANES, :]
    u_buf[:, 0:CF_HALO, :] = u_buf[:, ts:ts + CF_HALO, :]


def _even_mixer(x, g, w_in, sc_w, cf_w, cf_b, ln_g, ln_b, w_out, ts=SEQ_TILE):
    B, S, D = x.shape
    ev_in = w_in.shape[1]
    return pl.pallas_call(
        functools.partial(_even_kernel, ts=ts),
        grid=(B, S // ts),
        in_specs=[
            _tile_spec(D, ts),
            _const_spec((1, D)),
            _const_spec((D, ev_in)),
            _const_spec(sc_w.shape),
            _const_spec(cf_w.shape),
            _const_spec(cf_b.shape),
            _const_spec(ln_g.shape),
            _const_spec(ln_b.shape),
            _const_spec(w_out.shape),
        ],
        out_specs=_tile_spec(D, ts),
        out_shape=jax.ShapeDtypeStruct((B, S, D), F32),
        scratch_shapes=[
            pltpu.VMEM((ts, D), BF16),
            pltpu.VMEM((ev_in // SC_WIDTH, ts, SC_WIDTH), F32),
            pltpu.VMEM((SC_WIDTH // LANES, ts + SUBLANES, LANES), F32),
            pltpu.VMEM((CF_WIDTH // LANES, ts + CF_HALO, LANES), F32),
            pltpu.VMEM((ts, SC_WIDTH + CF_WIDTH), BF16),
        ],
        compiler_params=pltpu.CompilerParams(
            dimension_semantics=("arbitrary", "arbitrary"), vmem_limit_bytes=VMEM_LIMIT),
        name="even_mixer",
    )(x, g, w_in, sc_w, cf_w, cf_b, ln_g, ln_b, w_out)


def _weight_chunk_copy(w_hbm, layer, c, stage, sems, cols):
    slot = c % 2
    return pltpu.make_async_copy(w_hbm.at[layer, pl.ds(c * W_STAGE_ROWS, W_STAGE_ROWS), :],
                                 stage.at[slot, :, 0:cols], sems.at[slot])


def _load_bf16_weights(w_hbm, layer, dst_ref, stage, sems):
    rows, cols = dst_ref.shape
    n = rows // W_STAGE_ROWS
    for c in range(min(2, n)):
        _weight_chunk_copy(w_hbm, layer, c, stage, sems, cols).start()
    for c in range(n):
        _weight_chunk_copy(w_hbm, layer, c, stage, sems, cols).wait()
        dst_ref[c * W_STAGE_ROWS:(c + 1) * W_STAGE_ROWS, :] = stage[c % 2, :, 0:cols].astype(BF16)
        if c + 2 < n:
            _weight_chunk_copy(w_hbm, layer, c + 2, stage, sems, cols).start()


def _ffn_body(x, layer, ng_ref, wg_hbm, wu_hbm, cw_ref, cb_ref, wd_hbm, o_ref,
              h_buf, g_buf, up_buf, act_buf, wg_ref, wu_ref, wd_ref, stage, sems, ts):
    @pl.when((pl.program_id(0) == 0) & (pl.program_id(1) == 0))
    def _():
        for w_hbm, dst_ref in ((wg_hbm, wg_ref), (wu_hbm, wu_ref), (wd_hbm, wd_ref)):
            _load_bf16_weights(w_hbm, layer, dst_ref, stage, sems)

    @pl.when(pl.program_id(1) == 0)
    def _():
        g_buf[:, 0:SUBLANES, :] = jnp.zeros((D_FF // LANES, SUBLANES, LANES), F32)

    def g_rows(c0, start):
        return jnp.concatenate([g_buf[(c0 + l) // LANES, start:start + ROW_CHUNK, :]
                                for l in range(0, FFN_COL_CHUNK, LANES)], axis=1)

    h_buf[...] = _rms_bf16(x, ng_ref[...])
    for c0 in range(0, D_FF, FFN_COL_CHUNK):
        cols = slice(c0, c0 + FFN_COL_CHUNK)
        g = jnp.dot(h_buf[...], wg_ref[:, cols], preferred_element_type=F32)
        for l in range(0, FFN_COL_CHUNK, LANES):
            g_buf[(c0 + l) // LANES, SUBLANES:SUBLANES + ts, :] = g[:, l:l + LANES]
        up_buf[:, cols] = jnp.dot(h_buf[...], wu_ref[:, cols], preferred_element_type=F32)
        w0, w1, w2 = (_bcast_rows(cw_ref, k, cols, ROW_CHUNK) for k in range(3))
        bias = _bcast_rows(cb_ref, 0, cols, ROW_CHUNK)
        for r in range(0, ts, ROW_CHUNK):
            gc = (w0 * g_rows(c0, r + SUBLANES - 2) + w1 * g_rows(c0, r + SUBLANES - 1)
                  + w2 * g_rows(c0, r + SUBLANES) + bias)
            act_buf[r:r + ROW_CHUNK, cols] = (gc * _sigmoid(gc) * up_buf[r:r + ROW_CHUNK, cols]).astype(BF16)
    o_ref[0] = x + jnp.dot(act_buf[...], wd_ref[...], preferred_element_type=F32)
    g_buf[:, 0:SUBLANES, :] = g_buf[:, ts:ts + SUBLANES, :]


def _ffn_kernel(x_ref, ng_ref, wg_hbm, wu_hbm, cw_ref, cb_ref, wd_hbm, o_ref,
                *scratch, ts, layer):
    _ffn_body(x_ref[0], layer, ng_ref, wg_hbm, wu_hbm, cw_ref, cb_ref, wd_hbm, o_ref, *scratch, ts)


def _proj_ffn_kernel(x_ref, ya_ref, yb_ref, wo_ref, ng_ref, wg_hbm, wu_hbm, cw_ref, cb_ref, wd_hbm, o_ref,
                     *scratch, ts, layer):
    wa = ya_ref.shape[-1]
    yb = jnp.concatenate([yb_ref[0, c] for c in range(yb_ref.shape[1])], axis=1)
    x = x_ref[0] + jnp.dot(ya_ref[0].astype(BF16), wo_ref[0:wa, :], preferred_element_type=F32)
    x = x + jnp.dot(yb.astype(BF16), wo_ref[wa:, :], preferred_element_type=F32)
    _ffn_body(x, layer, ng_ref, wg_hbm, wu_hbm, cw_ref, cb_ref, wd_hbm, o_ref, *scratch, ts)


def _conv_ffn(x, layer, ng, wg, wu, cw, cb, wd, mix=None, ts=SEQ_TILE):
    B, S, D = x.shape
    in_hbm = pl.BlockSpec(memory_space=pl.ANY)
    ffn_specs = [_const_spec((1, D)), in_hbm, in_hbm, _const_spec(cw.shape), _const_spec(cb.shape), in_hbm]
    assert D % W_STAGE_ROWS == 0 and D_FF % W_STAGE_ROWS == 0
    if mix is None:
        body, ins, specs = _ffn_kernel, (x,), [_tile_spec(D, ts)]
    else:
        ya, yb, wo = mix
        body, ins = _proj_ffn_kernel, (x, ya, yb, wo)
        specs = [_tile_spec(D, ts), _tile_spec(ya.shape[-1], ts), _lane_split_spec(yb.shape[1] * LANES, ts),
                 _const_spec(wo.shape)]
    return pl.pallas_call(
        functools.partial(body, ts=ts, layer=layer),
        grid=(B, S // ts),
        in_specs=specs + ffn_specs,
        out_specs=_tile_spec(D, ts),
        out_shape=jax.ShapeDtypeStruct((B, S, D), F32),
        scratch_shapes=[
            pltpu.VMEM((ts, D), BF16),
            pltpu.VMEM((D_FF // LANES, ts + SUBLANES, LANES), F32),
            pltpu.VMEM((ts, D_FF), F32),
            pltpu.VMEM((ts, D_FF), BF16),
            pltpu.VMEM((D, D_FF), BF16),
            pltpu.VMEM((D, D_FF), BF16),
            pltpu.VMEM((D_FF, D), BF16),
            pltpu.VMEM((2, W_STAGE_ROWS, max(D, D_FF)), F32),
            pltpu.SemaphoreType.DMA((2,)),
        ],
        compiler_params=pltpu.CompilerParams(
            dimension_semantics=("arbitrary", "arbitrary"), vmem_limit_bytes=VMEM_LIMIT),
        name="conv_ffn" if mix is None else "proj_conv_ffn",
    )(*ins, ng, wg, wu, cw, cb, wd)


def _odd_in_kernel(x_ref, g_ref, win_ref, cw_ref, cb_ref, wa_ref, ba_ref, wx_ref, bx_ref, lam_ref,
                   qg_ref, kg_ref, seg_ref,
                   ylru_ref, qkv0_ref, qkv1_ref, qkv2_ref,
                   h_buf, xin_buf, xc_buf, xcb_buf, ga_buf, gx_buf, gate_buf, t_buf, nq_buf, a_buf, b_buf,
                   al_buf, bl_buf, hin_buf, h_carry, *, ts):
    qkv_refs = (qkv0_ref, qkv1_ref, qkv2_ref)
    @pl.when(pl.program_id(1) == 0)
    def _():
        xin_buf[:, 0:SUBLANES, :] = jnp.zeros((LRU_WIDTH // LANES, SUBLANES, LANES), F32)
        h_carry[...] = jnp.zeros((SUBLANES, LRU_WIDTH), F32)

    all_cols = slice(None)
    h_buf[...] = _rms_bf16(x_ref[0], g_ref[...])

    def proj(c0, width):
        return jnp.dot(h_buf[...], win_ref[:, c0:c0 + width], preferred_element_type=F32)

    def lru_input():
        _store_lanes(xin_buf, SUBLANES, proj(0, LRU_WIDTH))
        for r in range(0, ts, ROW_CHUNK):
            acc = _bcast_rows(cb_ref, 0, all_cols, ROW_CHUNK)
            for k in range(LRU_TAPS):
                acc += (_bcast_rows(cw_ref, k, all_cols, ROW_CHUNK)
                        * _load_lanes(xin_buf, SUBLANES + r - (LRU_TAPS - 1) + k, ROW_CHUNK))
            xc_buf[r:r + ROW_CHUNK, :] = acc
            xcb_buf[r:r + ROW_CHUNK, :] = acc.astype(BF16)
        xin_buf[:, 0:SUBLANES, :] = xin_buf[:, ts:ts + SUBLANES, :]

    def lru_gates():
        for w_ref, buf in ((wa_ref, ga_buf), (wx_ref, gx_buf)):
            for c0 in range(0, LRU_WIDTH, GATE_BLOCK):
                cols = slice(c0, c0 + GATE_BLOCK)
                buf[:, cols] = jnp.dot(xcb_buf[:, cols], w_ref[cols, cols], preferred_element_type=F32)

    def lru_gate_proj():
        gate_buf[...] = proj(LRU_WIDTH, LRU_WIDTH)

    nlam = -lam_ref[...]
    softplus = jnp.maximum(nlam, 0.0) + jnp.log(1.0 + jnp.exp(-jnp.abs(nlam)))
    decay = jnp.broadcast_to(-LRU_C * softplus, (SUBLANES, LRU_WIDTH))
    row = lax.broadcasted_iota(jnp.int32, (SUBLANES, LRU_WIDTH), 0)

    def scan_group(g):
        rows = slice(g * SUBLANES, (g + 1) * SUBLANES)
        rg = _sigmoid(ga_buf[rows, :] + ba_ref[...])
        ig = _sigmoid(gx_buf[rows, :] + bx_ref[...])
        a = jnp.exp(decay * rg)
        b = jnp.sqrt(1.0 - a * a) * (ig * xc_buf[rows, :])
        for s in (1, 2, 4):
            a_prev = jnp.where(row >= s, pltpu.roll(a, s, 0), 1.0)
            b_prev = jnp.where(row >= s, pltpu.roll(b, s, 0), 0.0)
            b = a * b_prev + b
            a = a * a_prev
        a_buf[rows, :] = a
        b_buf[rows, :] = b
        al_buf[rows, :] = jnp.broadcast_to(a[SUBLANES - 1:SUBLANES, :], (SUBLANES, LRU_WIDTH))
        bl_buf[rows, :] = jnp.broadcast_to(b[SUBLANES - 1:SUBLANES, :], (SUBLANES, LRU_WIDTH))

    half = ts // 2
    q0 = 2 * LRU_WIDTH
    qk_rows = 4 * ROW_CHUNK
    norm_args = ((qg_ref, DIL_HEAD_DIM ** -0.5 * math.log2(math.e)), (kg_ref, 1.0))
    v_index = len(norm_args)
    n_split = DIL_OUT // LANES

    def qkv_proj(which, h):
        rows = slice(h * half, (h + 1) * half)
        c0 = q0 + which * DIL_QKV
        z = jnp.dot(h_buf[rows, :], win_ref[:, c0:c0 + DIL_QKV], preferred_element_type=F32)
        if which == v_index:
            for c in range(DIL_QKV // LANES):
                nq_buf[which, c, rows, :] = z[:, c * LANES:(c + 1) * LANES]
        else:
            t_buf[which, rows, :] = z

    def qk_norm(which, h):
        gain_ref, scale = norm_args[which]
        for r in range(h * half, (h + 1) * half, qk_rows):
            for g0 in range(0, DIL_QKV, DIL_OUT):
                t = t_buf[which, r:r + qk_rows, g0:g0 + DIL_OUT]
                ss = jnp.dot((t * t).astype(BF16), seg_ref[...], preferred_element_type=F32)
                gain = _bcast_rows(gain_ref, 0, slice(g0, g0 + DIL_OUT), qk_rows) * scale
                t = t * lax.rsqrt(ss * (1.0 / DIL_HEAD_DIM) + EPS) * gain
                for l in range(0, DIL_OUT, LANES):
                    nq_buf[which, (g0 + l) // LANES, r:r + qk_rows, :] = t[:, l:l + LANES]

    def scatter(which, h):
        for g, (_, d) in enumerate(DIL_PATTERNS):
            n = half // d
            for c in range(n_split):
                for r in range(d):
                    rows = slice(h * half, (h + 1) * half) if d == 1 else pl.ds(h * half + r, n, stride=d)
                    qkv_refs[g][0, which, c, r, h * n:(h + 1) * n, :] = (
                        nq_buf[which, g * n_split + c, rows, :].astype(BF16))

    lru_input()
    qkv_proj(0, 0)
    lru_gates()
    qkv_proj(1, 0)
    lru_gate_proj()
    qk_norm(0, 0)
    qkv_proj(v_index, 0)
    scatter(0, 0)
    qkv_proj(0, 1)
    qk_norm(1, 0)
    qkv_proj(1, 1)
    scatter(1, 0)
    scatter(v_index, 0)
    qkv_proj(v_index, 1)
    epilogues = {8: lambda: qk_norm(0, 1), 20: lambda: scatter(0, 1), 32: lambda: qk_norm(1, 1),
                 44: lambda: scatter(1, 1), 56: lambda: scatter(v_index, 1)}
    for g in range(ts // SUBLANES):
        scan_group(g)
        if g in epilogues:
            epilogues[g]()

    hc = h_carry[...]
    for r in range(0, ts, SUBLANES):
        rows = slice(r, r + SUBLANES)
        hin_buf[rows, :] = hc
        hc = al_buf[rows, :] * hc + bl_buf[rows, :]
    h_carry[...] = hc
    for r in range(0, ts, ROW_CHUNK):
        rows = slice(r, r + ROW_CHUNK)
        hs = a_buf[rows, :] * hin_buf[rows, :] + b_buf[rows, :]
        gate = gate_buf[rows, :]
        gelu = 0.5 * gate * (1.0 + jnp.tanh(math.sqrt(2.0 / math.pi) * (gate + 0.044715 * (gate * gate * gate))))
        ylru_ref[0, rows, :] = hs * gelu


def _odd_in(x, g, w_in, cw, cb, wa, ba, wx, bx, lam, qg, kg, seg, ts=SEQ_TILE):
    B, S, D = x.shape
    n_split = DIL_OUT // LANES
    return pl.pallas_call(
        functools.partial(_odd_in_kernel, ts=ts),
        grid=(B, S // ts),
        in_specs=[
            _tile_spec(D, ts),
            _const_spec((1, D)),
            _const_spec(w_in.shape),
            _const_spec(cw.shape),
            _const_spec(cb.shape),
            _const_spec(wa.shape),
            _const_spec(ba.shape),
            _const_spec(wx.shape),
            _const_spec(bx.shape),
            _const_spec((1, LRU_WIDTH)),
            _const_spec(qg.shape),
            _const_spec(kg.shape),
            _const_spec(seg.shape),
        ],
        out_specs=[_tile_spec(LRU_WIDTH, ts)] + [
            pl.BlockSpec((1, 3, n_split, d, ts // d, LANES), lambda b, j: (b, 0, 0, 0, j, 0))
            for _, d in DIL_PATTERNS],
        out_shape=[jax.ShapeDtypeStruct((B, S, LRU_WIDTH), F32)] + [
            jax.ShapeDtypeStruct((B, 3, n_split, d, S // d, LANES), BF16) for _, d in DIL_PATTERNS],
        scratch_shapes=[
            pltpu.VMEM((ts, D), BF16),
            pltpu.VMEM((LRU_WIDTH // LANES, ts + SUBLANES, LANES), F32),
            pltpu.VMEM((ts, LRU_WIDTH), F32),
            pltpu.VMEM((ts, LRU_WIDTH), BF16),
            pltpu.VMEM((ts, LRU_WIDTH), F32),
            pltpu.VMEM((ts, LRU_WIDTH), F32),
            pltpu.VMEM((ts, LRU_WIDTH), F32),
            pltpu.VMEM((2, ts, DIL_QKV), F32),
            pltpu.VMEM((3, DIL_QKV // LANES, ts, LANES), F32),
            pltpu.VMEM((ts, LRU_WIDTH), F32),
            pltpu.VMEM((ts, LRU_WIDTH), F32),
            pltpu.VMEM((ts, LRU_WIDTH), F32),
            pltpu.VMEM((ts, LRU_WIDTH), F32),
            pltpu.VMEM((ts, LRU_WIDTH), F32),
            pltpu.VMEM((SUBLANES, LRU_WIDTH), F32),
        ],
        compiler_params=pltpu.CompilerParams(
            dimension_semantics=("arbitrary", "arbitrary"), vmem_limit_bytes=VMEM_LIMIT),
        name="odd_in_lru",
    )(x, g, w_in, cw, cb, wa, ba, wx, bx, lam, qg, kg, seg)


def _attn_kernel(*refs, dilation, prev_dilation, last, seq):
    qkv_ref, refs = refs[0], refs[1:]
    if prev_dilation is not None:
        (prev_o_ref, prev_lse_ref), refs = refs[:2], refs[2:]
    if last:
        o_ref, bias_buf = refs
    else:
        o_ref, lse_ref, bias_buf = refs
    blk = ATTN_BLOCK
    nb = seq // dilation // blk
    n_split = DIL_OUT // LANES
    qi = lax.broadcasted_iota(jnp.int32, (DIL_HEADS * blk, 2 * blk), 0) % blk
    kj = lax.broadcasted_iota(jnp.int32, (DIL_HEADS * blk, 2 * blk), 1)
    for t in range(2):
        steps = qi - kj + t * blk
        bias_buf[t] = jnp.where((steps >= 0) & (steps <= blk), 0.0, -jnp.inf).astype(F32)

    lane_head = lax.broadcasted_iota(jnp.int32, (1, DIL_OUT), 1) // DIL_HEAD_DIM
    is_head = [lane_head == h for h in range(DIL_HEADS)]

    def fold(t):
        out = t[(DIL_HEADS - 1) * blk:DIL_HEADS * blk]
        for h in range(DIL_HEADS - 2, -1, -1):
            out = jnp.where(is_head[h], t[h * blk:(h + 1) * blk], out)
        return out

    def load(which, res, rows):
        return jnp.concatenate([qkv_ref[0, which, c, res, rows, :] for c in range(n_split)], axis=1)

    def body(idx, carry):
        res = idx // nb
        b = idx % nb
        kb = jnp.maximum(b - 1, 0)
        q_rows = pl.ds(pl.multiple_of(b * blk, blk), blk)
        k_rows = pl.ds(pl.multiple_of(kb * blk, blk), 2 * blk)
        qb = load(0, res, q_rows)
        zero = jnp.zeros_like(qb)
        qs = jnp.concatenate([jnp.where(is_head[h], qb, zero) for h in range(DIL_HEADS)], axis=0)
        s = lax.dot_general(qs, load(1, res, k_rows), (((1,), (1,)), ((), ())), preferred_element_type=F32)
        s = s + bias_buf[jnp.minimum(b, 1)]
        m = jnp.max(s, axis=-1, keepdims=True)
        p = jnp.exp2(s - m)
        l = jnp.sum(p, axis=-1, keepdims=True)
        o = jnp.dot(p.astype(BF16), load(2, res, k_rows), preferred_element_type=F32)
        o_tok, m_tok, l_tok = fold(o), fold(m), fold(l)
        o_tok = o_tok / l_tok
        lse = m_tok + jnp.log2(l_tok)
        if prev_dilation is not None:
            step = dilation // prev_dilation
            p_rows = pl.ds(b * (blk * step) + res // prev_dilation, blk, stride=step)
            p_res = res % prev_dilation
            prev_o = jnp.concatenate([prev_o_ref[0, c, p_res, p_rows, :] for c in range(n_split)], axis=1)
            prev_lse = jnp.concatenate([prev_lse_ref[0, c, p_res, p_rows, :] for c in range(n_split)], axis=1)
            top = jnp.maximum(prev_lse, lse)
            w_prev = jnp.exp2(prev_lse - top)
            w_new = jnp.exp2(lse - top)
            den = w_prev + w_new
            o_tok = (w_prev * prev_o + w_new * o_tok) / den
            lse = top + jnp.log2(den)
        if last:
            seq_rows = pl.ds(b * (blk * dilation) + res, blk, stride=dilation)
            for c in range(n_split):
                o_ref[0, c, seq_rows, :] = o_tok[:, c * LANES:(c + 1) * LANES]
        else:
            for c in range(n_split):
                o_ref[0, c, res, q_rows, :] = o_tok[:, c * LANES:(c + 1) * LANES]
                lse_ref[0, c, res, q_rows, :] = lse[:, c * LANES:(c + 1) * LANES]
        return carry

    lax.fori_loop(0, dilation * nb, body, 0, unroll=8)


def _dilated_attention(qkv_groups):
    def batch_block(shape):
        return pl.BlockSpec((1,) + shape[1:], lambda b: (b,) + (0,) * (len(shape) - 1))

    merged, prev_d = (), None
    for g, qkv in enumerate(qkv_groups):
        B, _, n_split, d, L, _ = qkv.shape
        assert DIL_PATTERNS[g] == (ATTN_BLOCK * d, d)
        last = g == len(qkv_groups) - 1
        if last:
            out_shape = [jax.ShapeDtypeStruct((B, n_split, d * L, LANES), F32)]
        else:
            out_shape = [jax.ShapeDtypeStruct((B, n_split, d, L, LANES), F32)] * 2
        ins = (qkv,) + tuple(merged)
        merged = pl.pallas_call(
            functools.partial(_attn_kernel, dilation=d, prev_dilation=prev_d, last=last, seq=d * L),
            grid=(B,),
            in_specs=[batch_block(a.shape) for a in ins],
            out_specs=[batch_block(s.shape) for s in out_shape],
            out_shape=out_shape,
            scratch_shapes=[pltpu.VMEM((2, DIL_HEADS * ATTN_BLOCK, 2 * ATTN_BLOCK), F32)],
            compiler_params=pltpu.CompilerParams(
                dimension_semantics=("arbitrary",), vmem_limit_bytes=VMEM_LIMIT),
            name=f"dilated_attention_d{d}",
        )(*ins)
        prev_d = d
    return merged[0]


def _block_diag(w):
    H, n, _ = w.shape
    eye = jnp.eye(H, dtype=w.dtype)
    return (eye[:, None, :, None] * w[:, :, None, :]).reshape(H * n, H * n)


def kernel(x, mix_norm_g, ffn_norm_g, ev_w_in, ev_sc_conv_w, ev_cf_conv_w, ev_cf_conv_b, ev_cf_ln_g, ev_cf_ln_b, ev_w_out, od_w_in, od_lru_conv_w, od_lru_conv_b, od_lru_wa, od_lru_ba, od_lru_wx, od_lru_bx, od_lru_lam, od_q_norm_g, od_k_norm_g, od_w_out, ffn_w_gate, ffn_w_up, ffn_conv_w, ffn_conv_b, ffn_w_down):
    row = lambda p: p.reshape(1, -1)
    bf = lambda w: w.astype(BF16)
    n_heads = DIL_QKV // DIL_HEAD_DIM
    head_of = jnp.arange(DIL_OUT) // DIL_HEAD_DIM
    seg = (head_of[:, None] == head_of[None, :]).astype(BF16)

    def ffn(x, layer, mix=None):
        return _conv_ffn(x, layer, row(ffn_norm_g[layer]), ffn_w_gate, ffn_w_up,
                         _rep8(ffn_conv_w[layer]), _rep8(ffn_conv_b[layer]), ffn_w_down, mix=mix)

    x = _even_mixer(x, row(mix_norm_g[0]), bf(ev_w_in[0]), _rep8(ev_sc_conv_w[0]), _rep8(ev_cf_conv_w[0]),
                    _rep8(ev_cf_conv_b[0]), _rep8(ev_cf_ln_g[0]), _rep8(ev_cf_ln_b[0]), bf(ev_w_out[0]))
    x = ffn(x, 0)
    y_lru, *qkv_groups = _odd_in(
        x, row(mix_norm_g[1]), bf(od_w_in[0]), _rep8(od_lru_conv_w[0]), _rep8(od_lru_conv_b[0]),
        bf(_block_diag(od_lru_wa[0])), _rep8(od_lru_ba[0]), bf(_block_diag(od_lru_wx[0])), _rep8(od_lru_bx[0]),
        row(od_lru_lam[0]), _rep8(jnp.tile(od_q_norm_g[0], n_heads)), _rep8(jnp.tile(od_k_norm_g[0], n_heads)), seg)
    y_att = _dilated_attention(qkv_groups)
    return ffn(x, 1, mix=(y_lru, y_att, bf(od_w_out[0])))
```

```python
import functools
import math

import jax
import jax.numpy as jnp
from jax import lax
from jax.experimental import pallas as pl
from jax.experimental.pallas import tpu as pltpu

F32 = jnp.float32
BF16 = jnp.bfloat16

D_MODEL = 1024
SC_WIDTH = 512
SC_TAPS = 3
CF_WIDTH = 512
CF_TAPS = 31
LRU_WIDTH = 512
LRU_TAPS = 4
LRU_C = 8.0
MXU_WIDTH = 256
GATE_BLOCK = MXU_WIDTH
DIL_PATTERNS = ((128, 1), (512, 4), (2048, 16))
DIL_HEADS = 4
DIL_HEAD_DIM = 64
DIL_OUT = DIL_HEADS * DIL_HEAD_DIM
DIL_QKV = len(DIL_PATTERNS) * DIL_OUT
ATTN_BLOCK = 128
D_FF = 2816
EPS = 1e-6

SUBLANES = 8
LANES = 128
SEQ_TILE = 512
CF_HALO = 32
ROW_CHUNK = 32
FFN_COL_CHUNK = 256
VMEM_LIMIT = 56 * 1024 * 1024


def _rms_bf16(x, g):
    y = x * lax.rsqrt(jnp.mean(x * x, axis=-1, keepdims=True) + EPS)
    return (y * g).astype(BF16)


def _sigmoid(x):
    return 1.0 / (1.0 + jnp.exp(-x))


def _rep8(p):
    return jnp.repeat(p.reshape(-1, p.shape[-1]), SUBLANES, axis=0)


def _bcast_rows(ref, k, cols, n):
    w = ref[k * SUBLANES:(k + 1) * SUBLANES, cols]
    return jnp.concatenate([w] * (n // SUBLANES), axis=0)


def _load_lanes(ref, start, n):
    return jnp.concatenate([ref[c, start:start + n, :] for c in range(ref.shape[0])], axis=1)


def _lane_conv(ref, w_ref, n_taps, start, n):
    out = []
    for c in range(ref.shape[0]):
        cols = slice(c * LANES, (c + 1) * LANES)
        acc = None
        for k in range(n_taps):
            term = _bcast_rows(w_ref, k, cols, n) * ref[c, start + k:start + k + n, :]
            acc = term if acc is None else acc + term
        out.append(acc)
    return jnp.concatenate(out, axis=1)


def _store_lanes(ref, start, val):
    for c in range(ref.shape[0]):
        ref[c, start:start + val.shape[0], :] = val[:, c * LANES:(c + 1) * LANES]


def _const_spec(shape):
    return pl.BlockSpec(shape, lambda *_: (0,) * len(shape), pipeline_mode=pl.Buffered(1))


def _tile_spec(width, ts):
    return pl.BlockSpec((1, ts, width), lambda b, j: (b, j, 0))


def _lane_split_spec(width, ts):
    return pl.BlockSpec((1, width // LANES, ts, LANES), lambda b, j: (b, 0, j, 0))


def _even_kernel(x_ref, g_ref, win_ref, scw_ref, cfw_ref, cfb_ref, lng_ref, lnb_ref, wout_ref,
                 o_ref, h_buf, z_buf, cx_buf, u_buf, cat_buf, *, ts):
    @pl.when(pl.program_id(1) == 0)
    def _():
        cx_buf[:, 0:SUBLANES, :] = jnp.zeros((SC_WIDTH // LANES, SUBLANES, LANES), F32)
        u_buf[:, 0:CF_HALO, :] = jnp.zeros((CF_WIDTH // LANES, CF_HALO, LANES), F32)

    sc_b, sc_c, sc_x, cf_a, cf_g = range(5)
    all_cols = slice(None)
    half, quarter = ts // 2, ts // 4
    n_chunks = ts // ROW_CHUNK
    per_quarter = n_chunks // 4
    h_buf[...] = _rms_bf16(x_ref[0], g_ref[...])

    def proj_rows(c, h):
        rows = slice(h * half, (h + 1) * half)
        z_buf[c, rows, :] = jnp.dot(h_buf[rows, :], win_ref[:, c * SC_WIDTH:(c + 1) * SC_WIDTH],
                                    preferred_element_type=F32)

    def proj_cols(c, h):
        cols = slice(h * MXU_WIDTH, (h + 1) * MXU_WIDTH)
        z_buf[c, :, cols] = jnp.dot(h_buf[...], win_ref[:, c * SC_WIDTH + cols.start:c * SC_WIDTH + cols.stop],
                                    preferred_element_type=F32)

    def stage_u(k):
        rows = slice(k * ROW_CHUNK, (k + 1) * ROW_CHUNK)
        _store_lanes(u_buf, CF_HALO + rows.start, z_buf[cf_a, rows, :] * _sigmoid(z_buf[cf_g, rows, :]))

    def stage_cx(k):
        rows = slice(k * ROW_CHUNK, (k + 1) * ROW_CHUNK)
        _store_lanes(cx_buf, SUBLANES + rows.start, z_buf[sc_c, rows, :] * z_buf[sc_x, rows, :])

    def conv_sc(k):
        r = k * ROW_CHUNK
        acc = _lane_conv(cx_buf, scw_ref, SC_TAPS, SUBLANES + r - (SC_TAPS - 1), ROW_CHUNK)
        cat_buf[r:r + ROW_CHUNK, 0:SC_WIDTH] = (z_buf[sc_b, r:r + ROW_CHUNK, :] * acc).astype(BF16)

    def conv_cf(k):
        r = k * ROW_CHUNK
        u = _lane_conv(u_buf, cfw_ref, CF_TAPS, CF_HALO + r - (CF_TAPS - 1), ROW_CHUNK)
        u += _bcast_rows(cfb_ref, 0, all_cols, ROW_CHUNK)
        mu = jnp.mean(u, axis=-1, keepdims=True)
        uc = u - mu
        var = jnp.mean(uc * uc, axis=-1, keepdims=True)
        y = (uc * lax.rsqrt(var + EPS) * _bcast_rows(lng_ref, 0, all_cols, ROW_CHUNK)
             + _bcast_rows(lnb_ref, 0, all_cols, ROW_CHUNK))
        cat_buf[r:r + ROW_CHUNK, SC_WIDTH:SC_WIDTH + CF_WIDTH] = (y * _sigmoid(y)).astype(BF16)

    def out_proj(q):
        rows = slice(q * quarter, (q + 1) * quarter)
        for k in range(q * per_quarter, (q + 1) * per_quarter):
            conv_sc(k)
        o_ref[0, rows, :] = x_ref[0, rows, :] + jnp.dot(
            cat_buf[rows, :], wout_ref[...], preferred_element_type=F32)

    proj_rows(cf_a, 0)
    proj_rows(cf_g, 0)
    for k in range(n_chunks // 2):
        stage_u(k)
    after_chunk = {
        0: [lambda: proj_rows(cf_a, 1)],
        1: [lambda: proj_rows(cf_g, 1)] + [functools.partial(stage_u, k) for k in range(n_chunks // 2, n_chunks)],
        2: [lambda: proj_cols(sc_c, 0)], 3: [lambda: proj_cols(sc_c, 1)],
        4: [lambda: proj_cols(sc_x, 0)], 5: [lambda: proj_cols(sc_x, 1)],
        6: [lambda: proj_cols(sc_b, 0)], 7: [lambda: proj_cols(sc_b, 1)],
        8: [functools.partial(stage_cx, k) for k in range(n_chunks)],
        9: [lambda: out_proj(0)], 10: [lambda: out_proj(1)], 12: [lambda: out_proj(2)], 15: [lambda: out_proj(3)],
    }
    assert n_chunks == 16
    for k in range(n_chunks):
        conv_cf(k)
        for piece in after_chunk.get(k, ()):
            piece()

    cx_buf[:, 0:SUBLANES, :] = cx_buf[:, ts:ts + SUBLANES, :]
    u_buf[:, 0:CF_HALO, :] = u_buf[:, ts:ts + CF_HALO, :]


def _even_mixer(x, g, w_in, sc_w, cf_w, cf_b, ln_g, ln_b, w_out, ts=SEQ_TILE):
    B, S, D = x.shape
    ev_in = w_in.shape[1]
    return pl.pallas_call(
        functools.partial(_even_kernel, ts=ts),
        grid=(B, S // ts),
        in_specs=[
            _tile_spec(D, ts),
            _const_spec((1, D)),
            _const_spec((D, ev_in)),
            _const_spec(sc_w.shape),
            _const_spec(cf_w.shape),
            _const_spec(cf_b.shape),
            _const_spec(ln_g.shape),
            _const_spec(ln_b.shape),
            _const_spec(w_out.shape),
        ],
        out_specs=_tile_spec(D, ts),
        out_shape=jax.ShapeDtypeStruct((B, S, D), F32),
        scratch_shapes=[
            pltpu.VMEM((ts, D), BF16),
            pltpu.VMEM((ev_in // SC_WIDTH, ts, SC_WIDTH), F32),
            pltpu.VMEM((SC_WIDTH // LANES, ts + SUBLANES, LANES), F32),
            pltpu.VMEM((CF_WIDTH // LANES, ts + CF_HALO, LANES), F32),
            pltpu.VMEM((ts, SC_WIDTH + CF_WIDTH), BF16),
        ],
        compiler_params=pltpu.CompilerParams(
            dimension_semantics=("arbitrary", "arbitrary"), vmem_limit_bytes=VMEM_LIMIT),
        name="even_mixer",
    )(x, g, w_in, sc_w, cf_w, cf_b, ln_g, ln_b, w_out)


def _ffn_body(x, ng_ref, wg_ref, wu_ref, cw_ref, cb_ref, wd_ref, o_ref, h_buf, g_buf, up_buf, act_buf, ts):
    @pl.when(pl.program_id(1) == 0)
    def _():
        g_buf[:, 0:SUBLANES, :] = jnp.zeros((D_FF // LANES, SUBLANES, LANES), F32)

    def g_rows(c0, start):
        return jnp.concatenate([g_buf[(c0 + l) // LANES, start:start + ROW_CHUNK, :]
                                for l in range(0, FFN_COL_CHUNK, LANES)], axis=1)

    h_buf[...] = _rms_bf16(x, ng_ref[...])
    for c0 in range(0, D_FF, FFN_COL_CHUNK):
        cols = slice(c0, c0 + FFN_COL_CHUNK)
        g = jnp.dot(h_buf[...], wg_ref[:, cols], preferred_element_type=F32)
        for l in range(0, FFN_COL_CHUNK, LANES):
            g_buf[(c0 + l) // LANES, SUBLANES:SUBLANES + ts, :] = g[:, l:l + LANES]
        up_buf[:, cols] = jnp.dot(h_buf[...], wu_ref[:, cols], preferred_element_type=F32)
        w0, w1, w2 = (_bcast_rows(cw_ref, k, cols, ROW_CHUNK) for k in range(3))
        bias = _bcast_rows(cb_ref, 0, cols, ROW_CHUNK)
        for r in range(0, ts, ROW_CHUNK):
            gc = (w0 * g_rows(c0, r + SUBLANES - 2) + w1 * g_rows(c0, r + SUBLANES - 1)
                  + w2 * g_rows(c0, r + SUBLANES) + bias)
            act_buf[r:r + ROW_CHUNK, cols] = (gc * _sigmoid(gc) * up_buf[r:r + ROW_CHUNK, cols]).astype(BF16)
    o_ref[0] = x + jnp.dot(act_buf[...], wd_ref[...], preferred_element_type=F32)
    g_buf[:, 0:SUBLANES, :] = g_buf[:, ts:ts + SUBLANES, :]


def _ffn_kernel(x_ref, ng_ref, wg_ref, wu_ref, cw_ref, cb_ref, wd_ref, o_ref,
                *scratch, ts):
    _ffn_body(x_ref[0], ng_ref, wg_ref, wu_ref, cw_ref, cb_ref, wd_ref, o_ref, *scratch, ts)


def _proj_ffn_kernel(x_ref, ya_ref, yb_ref, wo_ref, ng_ref, wg_ref, wu_ref, cw_ref, cb_ref, wd_ref, o_ref,
                     *scratch, ts):
    wa = ya_ref.shape[-1]
    yb = jnp.concatenate([yb_ref[0, c] for c in range(yb_ref.shape[1])], axis=1)
    x = x_ref[0] + jnp.dot(ya_ref[0].astype(BF16), wo_ref[0:wa, :], preferred_element_type=F32)
    x = x + jnp.dot(yb.astype(BF16), wo_ref[wa:, :], preferred_element_type=F32)
    _ffn_body(x, ng_ref, wg_ref, wu_ref, cw_ref, cb_ref, wd_ref, o_ref, *scratch, ts)


def _layer_spec(stacked, layer):
    zeros = (0,) * (stacked.ndim - 1)
    return pl.BlockSpec((None,) + stacked.shape[1:], lambda *_: (layer,) + zeros, pipeline_mode=pl.Buffered(1))


def _conv_ffn(x, layer, ng, wg, wu, cw, cb, wd, mix=None, ts=SEQ_TILE):
    B, S, D = x.shape
    ffn_specs = [
        _const_spec((1, D)),
        _layer_spec(wg, layer),
        _layer_spec(wu, layer),
        _const_spec(cw.shape),
        _const_spec(cb.shape),
        _layer_spec(wd, layer),
    ]
    if mix is None:
        body, ins, specs = _ffn_kernel, (x,), [_tile_spec(D, ts)]
    else:
        ya, yb, wo = mix
        body, ins = _proj_ffn_kernel, (x, ya, yb, wo)
        specs = [_tile_spec(D, ts), _tile_spec(ya.shape[-1], ts), _lane_split_spec(yb.shape[1] * LANES, ts),
                 _const_spec(wo.shape)]
    return pl.pallas_call(
        functools.partial(body, ts=ts),
        grid=(B, S // ts),
        in_specs=specs + ffn_specs,
        out_specs=_tile_spec(D, ts),
        out_shape=jax.ShapeDtypeStruct((B, S, D), F32),
        scratch_shapes=[
            pltpu.VMEM((ts, D), BF16),
            pltpu.VMEM((D_FF // LANES, ts + SUBLANES, LANES), F32),
            pltpu.VMEM((ts, D_FF), F32),
            pltpu.VMEM((ts, D_FF), BF16),
        ],
        compiler_params=pltpu.CompilerParams(
            dimension_semantics=("arbitrary", "arbitrary"), vmem_limit_bytes=VMEM_LIMIT),
        name="conv_ffn" if mix is None else "proj_conv_ffn",
    )(*ins, ng, wg, wu, cw, cb, wd)


def _odd_in_kernel(x_ref, g_ref, win_ref, cw_ref, cb_ref, wa_ref, ba_ref, wx_ref, bx_ref, lam_ref,
                   qg_ref, kg_ref, seg_ref,
                   ylru_ref, qkv0_ref, qkv1_ref, qkv2_ref,
                   h_buf, xin_buf, xc_buf, xcb_buf, ga_buf, gx_buf, gate_buf, t_buf, nq_buf, a_buf, b_buf,
                   al_buf, bl_buf, hin_buf, h_carry, *, ts):
    qkv_refs = (qkv0_ref, qkv1_ref, qkv2_ref)
    @pl.when(pl.program_id(1) == 0)
    def _():
        xin_buf[:, 0:SUBLANES, :] = jnp.zeros((LRU_WIDTH // LANES, SUBLANES, LANES), F32)
        h_carry[...] = jnp.zeros((SUBLANES, LRU_WIDTH), F32)

    all_cols = slice(None)
    h_buf[...] = _rms_bf16(x_ref[0], g_ref[...])

    def proj(c0, width):
        return jnp.dot(h_buf[...], win_ref[:, c0:c0 + width], preferred_element_type=F32)

    def lru_input():
        _store_lanes(xin_buf, SUBLANES, proj(0, LRU_WIDTH))
        for r in range(0, ts, ROW_CHUNK):
            acc = _bcast_rows(cb_ref, 0, all_cols, ROW_CHUNK)
            for k in range(LRU_TAPS):
                acc += (_bcast_rows(cw_ref, k, all_cols, ROW_CHUNK)
                        * _load_lanes(xin_buf, SUBLANES + r - (LRU_TAPS - 1) + k, ROW_CHUNK))
            xc_buf[r:r + ROW_CHUNK, :] = acc
            xcb_buf[r:r + ROW_CHUNK, :] = acc.astype(BF16)
        xin_buf[:, 0:SUBLANES, :] = xin_buf[:, ts:ts + SUBLANES, :]

    def lru_gates():
        for w_ref, buf in ((wa_ref, ga_buf), (wx_ref, gx_buf)):
            for c0 in range(0, LRU_WIDTH, GATE_BLOCK):
                cols = slice(c0, c0 + GATE_BLOCK)
                buf[:, cols] = jnp.dot(xcb_buf[:, cols], w_ref[cols, cols], preferred_element_type=F32)

    def lru_gate_proj():
        gate_buf[...] = proj(LRU_WIDTH, LRU_WIDTH)

    nlam = -lam_ref[...]
    softplus = jnp.maximum(nlam, 0.0) + jnp.log(1.0 + jnp.exp(-jnp.abs(nlam)))
    decay = jnp.broadcast_to(-LRU_C * softplus, (SUBLANES, LRU_WIDTH))
    row = lax.broadcasted_iota(jnp.int32, (SUBLANES, LRU_WIDTH), 0)

    def scan_group(g):
        rows = slice(g * SUBLANES, (g + 1) * SUBLANES)
        rg = _sigmoid(ga_buf[rows, :] + ba_ref[...])
        ig = _sigmoid(gx_buf[rows, :] + bx_ref[...])
        a = jnp.exp(decay * rg)
        b = jnp.sqrt(1.0 - a * a) * (ig * xc_buf[rows, :])
        for s in (1, 2, 4):
            a_prev = jnp.where(row >= s, pltpu.roll(a, s, 0), 1.0)
            b_prev = jnp.where(row >= s, pltpu.roll(b, s, 0), 0.0)
            b = a * b_prev + b
            a = a * a_prev
        a_buf[rows, :] = a
        b_buf[rows, :] = b
        al_buf[rows, :] = jnp.broadcast_to(a[SUBLANES - 1:SUBLANES, :], (SUBLANES, LRU_WIDTH))
        bl_buf[rows, :] = jnp.broadcast_to(b[SUBLANES - 1:SUBLANES, :], (SUBLANES, LRU_WIDTH))

    half = ts // 2
    q0 = 2 * LRU_WIDTH
    qk_rows = 4 * ROW_CHUNK
    norm_args = ((qg_ref, DIL_HEAD_DIM ** -0.5 * math.log2(math.e)), (kg_ref, 1.0))
    v_index = len(norm_args)
    n_split = DIL_OUT // LANES

    def qkv_proj(which, h):
        rows = slice(h * half, (h + 1) * half)
        c0 = q0 + which * DIL_QKV
        z = jnp.dot(h_buf[rows, :], win_ref[:, c0:c0 + DIL_QKV], preferred_element_type=F32)
        if which == v_index:
            for c in range(DIL_QKV // LANES):
                nq_buf[which, c, rows, :] = z[:, c * LANES:(c + 1) * LANES]
        else:
            t_buf[which, rows, :] = z

    def qk_norm(which, h):
        gain_ref, scale = norm_args[which]
        for r in range(h * half, (h + 1) * half, qk_rows):
            for g0 in range(0, DIL_QKV, DIL_OUT):
                t = t_buf[which, r:r + qk_rows, g0:g0 + DIL_OUT]
                ss = jnp.dot((t * t).astype(BF16), seg_ref[...], preferred_element_type=F32)
                gain = _bcast_rows(gain_ref, 0, slice(g0, g0 + DIL_OUT), qk_rows) * scale
                t = t * lax.rsqrt(ss * (1.0 / DIL_HEAD_DIM) + EPS) * gain
                for l in range(0, DIL_OUT, LANES):
                    nq_buf[which, (g0 + l) // LANES, r:r + qk_rows, :] = t[:, l:l + LANES]

    def scatter(which, h):
        for g, (_, d) in enumerate(DIL_PATTERNS):
            n = half // d
            for c in range(n_split):
                for r in range(d):
                    rows = slice(h * half, (h + 1) * half) if d == 1 else pl.ds(h * half + r, n, stride=d)
                    qkv_refs[g][0, which, c, r, h * n:(h + 1) * n, :] = (
                        nq_buf[which, g * n_split + c, rows, :].astype(BF16))

    lru_input()
    qkv_proj(0, 0)
    lru_gates()
    qkv_proj(1, 0)
    lru_gate_proj()
    qk_norm(0, 0)
    qkv_proj(v_index, 0)
    scatter(0, 0)
    qkv_proj(0, 1)
    qk_norm(1, 0)
    qkv_proj(1, 1)
    scatter(1, 0)
    scatter(v_index, 0)
    qkv_proj(v_index, 1)
    epilogues = {8: lambda: qk_norm(0, 1), 20: lambda: scatter(0, 1), 32: lambda: qk_norm(1, 1),
                 44: lambda: scatter(1, 1), 56: lambda: scatter(v_index, 1)}
    for g in range(ts // SUBLANES):
        scan_group(g)
        if g in epilogues:
            epilogues[g]()

    hc = h_carry[...]
    for r in range(0, ts, SUBLANES):
        rows = slice(r, r + SUBLANES)
        hin_buf[rows, :] = hc
        hc = al_buf[rows, :] * hc + bl_buf[rows, :]
    h_carry[...] = hc
    for r in range(0, ts, ROW_CHUNK):
        rows = slice(r, r + ROW_CHUNK)
        hs = a_buf[rows, :] * hin_buf[rows, :] + b_buf[rows, :]
        gate = gate_buf[rows, :]
        gelu = 0.5 * gate * (1.0 + jnp.tanh(math.sqrt(2.0 / math.pi) * (gate + 0.044715 * (gate * gate * gate))))
        ylru_ref[0, rows, :] = hs * gelu


def _odd_in(x, g, w_in, cw, cb, wa, ba, wx, bx, lam, qg, kg, seg, ts=SEQ_TILE):
    B, S, D = x.shape
    n_split = DIL_OUT // LANES
    return pl.pallas_call(
        functools.partial(_odd_in_kernel, ts=ts),
        grid=(B, S // ts),
        in_specs=[
            _tile_spec(D, ts),
            _const_spec((1, D)),
            _const_spec(w_in.shape),
            _const_spec(cw.shape),
            _const_spec(cb.shape),
            _const_spec(wa.shape),
            _const_spec(ba.shape),
            _const_spec(wx.shape),
            _const_spec(bx.shape),
            _const_spec((1, LRU_WIDTH)),
            _const_spec(qg.shape),
            _const_spec(kg.shape),
            _const_spec(seg.shape),
        ],
        out_specs=[_tile_spec(LRU_WIDTH, ts)] + [
            pl.BlockSpec((1, 3, n_split, d, ts // d, LANES), lambda b, j: (b, 0, 0, 0, j, 0))
            for _, d in DIL_PATTERNS],
        out_shape=[jax.ShapeDtypeStruct((B, S, LRU_WIDTH), F32)] + [
            jax.ShapeDtypeStruct((B, 3, n_split, d, S // d, LANES), BF16) for _, d in DIL_PATTERNS],
        scratch_shapes=[
            pltpu.VMEM((ts, D), BF16),
            pltpu.VMEM((LRU_WIDTH // LANES, ts + SUBLANES, LANES), F32),
            pltpu.VMEM((ts, LRU_WIDTH), F32),
            pltpu.VMEM((ts, LRU_WIDTH), BF16),
            pltpu.VMEM((ts, LRU_WIDTH), F32),
            pltpu.VMEM((ts, LRU_WIDTH), F32),
            pltpu.VMEM((ts, LRU_WIDTH), F32),
            pltpu.VMEM((2, ts, DIL_QKV), F32),
            pltpu.VMEM((3, DIL_QKV // LANES, ts, LANES), F32),
            pltpu.VMEM((ts, LRU_WIDTH), F32),
            pltpu.VMEM((ts, LRU_WIDTH), F32),
            pltpu.VMEM((ts, LRU_WIDTH), F32),
            pltpu.VMEM((ts, LRU_WIDTH), F32),
            pltpu.VMEM((ts, LRU_WIDTH), F32),
            pltpu.VMEM((SUBLANES, LRU_WIDTH), F32),
        ],
        compiler_params=pltpu.CompilerParams(
            dimension_semantics=("arbitrary", "arbitrary"), vmem_limit_bytes=VMEM_LIMIT),
        name="odd_in_lru",
    )(x, g, w_in, cw, cb, wa, ba, wx, bx, lam, qg, kg, seg)


def _attn_kernel(*refs, dilation, prev_dilation, last, seq):
    qkv_ref, refs = refs[0], refs[1:]
    if prev_dilation is not None:
        (prev_o_ref, prev_lse_ref), refs = refs[:2], refs[2:]
    if last:
        o_ref, bias_buf = refs
    else:
        o_ref, lse_ref, bias_buf = refs
    blk = ATTN_BLOCK
    nb = seq // dilation // blk
    n_split = DIL_OUT // LANES
    qi = lax.broadcasted_iota(jnp.int32, (DIL_HEADS * blk, 2 * blk), 0) % blk
    kj = lax.broadcasted_iota(jnp.int32, (DIL_HEADS * blk, 2 * blk), 1)
    for t in range(2):
        steps = qi - kj + t * blk
        bias_buf[t] = jnp.where((steps >= 0) & (steps <= blk), 0.0, -jnp.inf).astype(F32)

    lane_head = lax.broadcasted_iota(jnp.int32, (1, DIL_OUT), 1) // DIL_HEAD_DIM
    is_head = [lane_head == h for h in range(DIL_HEADS)]

    def fold(t):
        out = t[(DIL_HEADS - 1) * blk:DIL_HEADS * blk]
        for h in range(DIL_HEADS - 2, -1, -1):
            out = jnp.where(is_head[h], t[h * blk:(h + 1) * blk], out)
        return out

    def load(which, res, rows):
        return jnp.concatenate([qkv_ref[0, which, c, res, rows, :] for c in range(n_split)], axis=1)

    def body(idx, carry):
        res = idx // nb
        b = idx % nb
        kb = jnp.maximum(b - 1, 0)
        q_rows = pl.ds(pl.multiple_of(b * blk, blk), blk)
        k_rows = pl.ds(pl.multiple_of(kb * blk, blk), 2 * blk)
        qb = load(0, res, q_rows)
        zero = jnp.zeros_like(qb)
        qs = jnp.concatenate([jnp.where(is_head[h], qb, zero) for h in range(DIL_HEADS)], axis=0)
        s = lax.dot_general(qs, load(1, res, k_rows), (((1,), (1,)), ((), ())), preferred_element_type=F32)
        s = s + bias_buf[jnp.minimum(b, 1)]
        m = jnp.max(s, axis=-1, keepdims=True)
        p = jnp.exp2(s - m)
        l = jnp.sum(p, axis=-1, keepdims=True)
        o = jnp.dot(p.astype(BF16), load(2, res, k_rows), preferred_element_type=F32)
        o_tok, m_tok, l_tok = fold(o), fold(m), fold(l)
        o_tok = o_tok / l_tok
        lse = m_tok + jnp.log2(l_tok)
        if prev_dilation is not None:
            step = dilation // prev_dilation
            p_rows = pl.ds(b * (blk * step) + res // prev_dilation, blk, stride=step)
            p_res = res % prev_dilation
            prev_o = jnp.concatenate([prev_o_ref[0, c, p_res, p_rows, :] for c in range(n_split)], axis=1)
            prev_lse = jnp.concatenate([prev_lse_ref[0, c, p_res, p_rows, :] for c in range(n_split)], axis=1)
            top = jnp.maximum(prev_lse, lse)
            w_prev = jnp.exp2(prev_lse - top)
            w_new = jnp.exp2(lse - top)
            den = w_prev + w_new
            o_tok = (w_prev * prev_o + w_new * o_tok) / den
            lse = top + jnp.log2(den)
        if last:
            seq_rows = pl.ds(b * (blk * dilation) + res, blk, stride=dilation)
            for c in range(n_split):
                o_ref[0, c, seq_rows, :] = o_tok[:, c * LANES:(c + 1) * LANES]
        else:
            for c in range(n_split):
                o_ref[0, c, res, q_rows, :] = o_tok[:, c * LANES:(c + 1) * LANES]
                lse_ref[0, c, res, q_rows, :] = lse[:, c * LANES:(c + 1) * LANES]
        return carry

    lax.fori_loop(0, dilation * nb, body, 0, unroll=16)


def _dilated_attention(qkv_groups):
    def batch_block(shape):
        return pl.BlockSpec((1,) + shape[1:], lambda b: (b,) + (0,) * (len(shape) - 1))

    merged, prev_d = (), None
    for g, qkv in enumerate(qkv_groups):
        B, _, n_split, d, L, _ = qkv.shape
        assert DIL_PATTERNS[g] == (ATTN_BLOCK * d, d)
        last = g == len(qkv_groups) - 1
        if last:
            out_shape = [jax.ShapeDtypeStruct((B, n_split, d * L, LANES), F32)]
        else:
            out_shape = [jax.ShapeDtypeStruct((B, n_split, d, L, LANES), F32)] * 2
        ins = (qkv,) + tuple(merged)
        merged = pl.pallas_call(
            functools.partial(_attn_kernel, dilation=d, prev_dilation=prev_d, last=last, seq=d * L),
            grid=(B,),
            in_specs=[batch_block(a.shape) for a in ins],
            out_specs=[batch_block(s.shape) for s in out_shape],
            out_shape=out_shape,
            scratch_shapes=[pltpu.VMEM((2, DIL_HEADS * ATTN_BLOCK, 2 * ATTN_BLOCK), F32)],
            compiler_params=pltpu.CompilerParams(
                dimension_semantics=("arbitrary",), vmem_limit_bytes=VMEM_LIMIT),
            name=f"dilated_attention_d{d}",
        )(*ins)
        prev_d = d
    return merged[0]


def _block_diag(w):
    H, n, _ = w.shape
    eye = jnp.eye(H, dtype=w.dtype)
    return (eye[:, None, :, None] * w[:, :, None, :]).reshape(H * n, H * n)


def kernel(x, mix_norm_g, ffn_norm_g, ev_w_in, ev_sc_conv_w, ev_cf_conv_w, ev_cf_conv_b, ev_cf_ln_g, ev_cf_ln_b, ev_w_out, od_w_in, od_lru_conv_w, od_lru_conv_b, od_lru_wa, od_lru_ba, od_lru_wx, od_lru_bx, od_lru_lam, od_q_norm_g, od_k_norm_g, od_w_out, ffn_w_gate, ffn_w_up, ffn_conv_w, ffn_conv_b, ffn_w_down):
    row = lambda p: p.reshape(1, -1)
    bf = lambda w: w.astype(BF16)
    n_heads = DIL_QKV // DIL_HEAD_DIM
    head_of = jnp.arange(DIL_OUT) // DIL_HEAD_DIM
    seg = (head_of[:, None] == head_of[None, :]).astype(BF16)

    ffn_wg, ffn_wu, ffn_wd = bf(ffn_w_gate), bf(ffn_w_up), bf(ffn_w_down)

    def ffn(x, layer, mix=None):
        return _conv_ffn(x, layer, row(ffn_norm_g[layer]), ffn_wg, ffn_wu,
                         _rep8(ffn_conv_w[layer]), _rep8(ffn_conv_b[layer]), ffn_wd, mix=mix)

    x = _even_mixer(x, row(mix_norm_g[0]), bf(ev_w_in[0]), _rep8(ev_sc_conv_w[0]), _rep8(ev_cf_conv_w[0]),
                    _rep8(ev_cf_conv_b[0]), _rep8(ev_cf_ln_g[0]), _rep8(ev_cf_ln_b[0]), bf(ev_w_out[0]))
    x = ffn(x, 0)
    y_lru, *qkv_groups = _odd_in(
        x, row(mix_norm_g[1]), bf(od_w_in[0]), _rep8(od_lru_conv_w[0]), _rep8(od_lru_conv_b[0]),
        bf(_block_diag(od_lru_wa[0])), _rep8(od_lru_ba[0]), bf(_block_diag(od_lru_wx[0])), _rep8(od_lru_bx[0]),
        row(od_lru_lam[0]), _rep8(jnp.tile(od_q_norm_g[0], n_heads)), _rep8(jnp.tile(od_k_norm_g[0], n_heads)), seg)
    y_att = _dilated_attention(qkv_groups)
    return ffn(x, 1, mix=(y_lru, y_att, bf(od_w_out[0])))
```

```python
import functools
import math

import jax
import jax.numpy as jnp
from jax import lax
from jax.experimental import pallas as pl
from jax.experimental.pallas import tpu as pltpu

F32 = jnp.float32
BF16 = jnp.bfloat16

D_MODEL = 1024
SC_WIDTH = 512
SC_TAPS = 3
CF_WIDTH = 512
CF_TAPS = 31
LRU_WIDTH = 512
LRU_TAPS = 4
LRU_C = 8.0
MXU_WIDTH = 256
GATE_BLOCK = MXU_WIDTH
DIL_PATTERNS = ((128, 1), (512, 4), (2048, 16))
DIL_HEADS = 4
DIL_HEAD_DIM = 64
DIL_OUT = DIL_HEADS * DIL_HEAD_DIM
DIL_QKV = len(DIL_PATTERNS) * DIL_OUT
ATTN_BLOCK = 128
D_FF = 2816
EPS = 1e-6

SUBLANES = 8
LANES = 128
MAX_SUBLANE_STRIDE = 4
SEQ_TILE = 512
CF_HALO = 32
ROW_CHUNK = 32
FFN_COL_CHUNK = 256
VMEM_LIMIT = 56 * 1024 * 1024


def _rms_bf16(x, g):
    y = x * lax.rsqrt(jnp.mean(x * x, axis=-1, keepdims=True) + EPS)
    return (y * g).astype(BF16)


def _sigmoid(x):
    return 1.0 / (1.0 + jnp.exp(-x))


def _rep8(p):
    return jnp.repeat(p.reshape(-1, p.shape[-1]), SUBLANES, axis=0)


def _bcast_rows(ref, k, cols, n):
    w = ref[k * SUBLANES:(k + 1) * SUBLANES, cols]
    return jnp.concatenate([w] * (n // SUBLANES), axis=0)


def _load_lanes(ref, start, n):
    return jnp.concatenate([ref[c, start:start + n, :] for c in range(ref.shape[0])], axis=1)


def _lane_conv(ref, w_ref, n_taps, start, n):
    out = []
    for c in range(ref.shape[0]):
        cols = slice(c * LANES, (c + 1) * LANES)
        acc = None
        for k in range(n_taps):
            term = _bcast_rows(w_ref, k, cols, n) * ref[c, start + k:start + k + n, :]
            acc = term if acc is None else acc + term
        out.append(acc)
    return jnp.concatenate(out, axis=1)


def _store_lanes(ref, start, val):
    for c in range(ref.shape[0]):
        ref[c, start:start + val.shape[0], :] = val[:, c * LANES:(c + 1) * LANES]


def _const_spec(shape):
    return pl.BlockSpec(shape, lambda *_: (0,) * len(shape), pipeline_mode=pl.Buffered(1))


def _tile_spec(width, ts):
    return pl.BlockSpec((1, ts, width), lambda b, j: (b, j, 0))


def _lane_split_spec(width, ts):
    return pl.BlockSpec((1, width // LANES, ts, LANES), lambda b, j: (b, 0, j, 0))


def _even_kernel(x_ref, g_ref, win_ref, scw_ref, cfw_ref, cfb_ref, lng_ref, lnb_ref, wout_ref,
                 o_ref, h_buf, z_buf, cx_buf, u_buf, cat_buf, *, ts):
    @pl.when(pl.program_id(1) == 0)
    def _():
        cx_buf[:, 0:SUBLANES, :] = jnp.zeros((SC_WIDTH // LANES, SUBLANES, LANES), F32)
        u_buf[:, 0:CF_HALO, :] = jnp.zeros((CF_WIDTH // LANES, CF_HALO, LANES), F32)

    sc_b, sc_c, sc_x, cf_a, cf_g = range(5)
    all_cols = slice(None)
    half, quarter = ts // 2, ts // 4
    n_chunks = ts // ROW_CHUNK
    per_quarter = n_chunks // 4
    h_buf[...] = _rms_bf16(x_ref[0], g_ref[...])

    def proj_rows(c, h):
        rows = slice(h * half, (h + 1) * half)
        z_buf[c, rows, :] = jnp.dot(h_buf[rows, :], win_ref[:, c * SC_WIDTH:(c + 1) * SC_WIDTH],
                                    preferred_element_type=F32)

    def proj_cols(c, h):
        cols = slice(h * MXU_WIDTH, (h + 1) * MXU_WIDTH)
        z_buf[c, :, cols] = jnp.dot(h_buf[...], win_ref[:, c * SC_WIDTH + cols.start:c * SC_WIDTH + cols.stop],
                                    preferred_element_type=F32)

    def stage_u(k):
        rows = slice(k * ROW_CHUNK, (k + 1) * ROW_CHUNK)
        _store_lanes(u_buf, CF_HALO + rows.start, z_buf[cf_a, rows, :] * _sigmoid(z_buf[cf_g, rows, :]))

    def stage_cx(k):
        rows = slice(k * ROW_CHUNK, (k + 1) * ROW_CHUNK)
        _store_lanes(cx_buf, SUBLANES + rows.start, z_buf[sc_c, rows, :] * z_buf[sc_x, rows, :])

    def conv_sc(k):
        r = k * ROW_CHUNK
        acc = _lane_conv(cx_buf, scw_ref, SC_TAPS, SUBLANES + r - (SC_TAPS - 1), ROW_CHUNK)
        cat_buf[r:r + ROW_CHUNK, 0:SC_WIDTH] = (z_buf[sc_b, r:r + ROW_CHUNK, :] * acc).astype(BF16)

    def conv_cf(k):
        r = k * ROW_CHUNK
        u = _lane_conv(u_buf, cfw_ref, CF_TAPS, CF_HALO + r - (CF_TAPS - 1), ROW_CHUNK)
        u += _bcast_rows(cfb_ref, 0, all_cols, ROW_CHUNK)
        mu = jnp.mean(u, axis=-1, keepdims=True)
        uc = u - mu
        var = jnp.mean(uc * uc, axis=-1, keepdims=True)
        y = (uc * lax.rsqrt(var + EPS) * _bcast_rows(lng_ref, 0, all_cols, ROW_CHUNK)
             + _bcast_rows(lnb_ref, 0, all_cols, ROW_CHUNK))
        cat_buf[r:r + ROW_CHUNK, SC_WIDTH:SC_WIDTH + CF_WIDTH] = (y * _sigmoid(y)).astype(BF16)

    def out_proj(q):
        rows = slice(q * quarter, (q + 1) * quarter)
        for k in range(q * per_quarter, (q + 1) * per_quarter):
            conv_sc(k)
        o_ref[0, rows, :] = x_ref[0, rows, :] + jnp.dot(
            cat_buf[rows, :], wout_ref[...], preferred_element_type=F32)

    proj_rows(cf_a, 0)
    proj_rows(cf_g, 0)
    for k in range(n_chunks // 2):
        stage_u(k)
    after_chunk = {
        0: [lambda: proj_rows(cf_a, 1)],
        1: [lambda: proj_rows(cf_g, 1)] + [functools.partial(stage_u, k) for k in range(n_chunks // 2, n_chunks)],
        2: [lambda: proj_cols(sc_c, 0)], 3: [lambda: proj_cols(sc_c, 1)],
        4: [lambda: proj_cols(sc_x, 0)], 5: [lambda: proj_cols(sc_x, 1)],
        6: [lambda: proj_cols(sc_b, 0)], 7: [lambda: proj_cols(sc_b, 1)],
        8: [functools.partial(stage_cx, k) for k in range(n_chunks)],
        9: [lambda: out_proj(0)], 10: [lambda: out_proj(1)], 12: [lambda: out_proj(2)], 15: [lambda: out_proj(3)],
    }
    assert n_chunks == 16
    for k in range(n_chunks):
        conv_cf(k)
        for piece in after_chunk.get(k, ()):
            piece()

    cx_buf[:, 0:SUBLANES, :] = cx_buf[:, ts:ts + SUBLANES, :]
    u_buf[:, 0:CF_HALO, :] = u_buf[:, ts:ts + CF_HALO, :]


def _even_mixer(x, g, w_in, sc_w, cf_w, cf_b, ln_g, ln_b, w_out, ts=SEQ_TILE):
    B, S, D = x.shape
    ev_in = w_in.shape[1]
    return pl.pallas_call(
        functools.partial(_even_kernel, ts=ts),
        grid=(B, S // ts),
        in_specs=[
            _tile_spec(D, ts),
            _const_spec((1, D)),
            _const_spec((D, ev_in)),
            _const_spec(sc_w.shape),
            _const_spec(cf_w.shape),
            _const_spec(cf_b.shape),
            _const_spec(ln_g.shape),
            _const_spec(ln_b.shape),
            _const_spec(w_out.shape),
        ],
        out_specs=_tile_spec(D, ts),
        out_shape=jax.ShapeDtypeStruct((B, S, D), F32),
        scratch_shapes=[
            pltpu.VMEM((ts, D), BF16),
            pltpu.VMEM((ev_in // SC_WIDTH, ts, SC_WIDTH), F32),
            pltpu.VMEM((SC_WIDTH // LANES, ts + SUBLANES, LANES), F32),
            pltpu.VMEM((CF_WIDTH // LANES, ts + CF_HALO, LANES), F32),
            pltpu.VMEM((ts, SC_WIDTH + CF_WIDTH), BF16),
        ],
        compiler_params=pltpu.CompilerParams(
            dimension_semantics=("arbitrary", "arbitrary"), vmem_limit_bytes=VMEM_LIMIT),
        name="even_mixer",
    )(x, g, w_in, sc_w, cf_w, cf_b, ln_g, ln_b, w_out)


def _ffn_body(x, ng_ref, wg_ref, wu_ref, cw_ref, cb_ref, wd_ref, o_ref, h_buf, g_buf, up_buf, act_buf, ts):
    @pl.when(pl.program_id(1) == 0)
    def _():
        g_buf[:, 0:SUBLANES, :] = jnp.zeros((D_FF // LANES, SUBLANES, LANES), F32)

    def g_rows(c0, start):
        return jnp.concatenate([g_buf[(c0 + l) // LANES, start:start + ROW_CHUNK, :]
                                for l in range(0, FFN_COL_CHUNK, LANES)], axis=1)

    h_buf[...] = _rms_bf16(x, ng_ref[...])
    for c0 in range(0, D_FF, FFN_COL_CHUNK):
        cols = slice(c0, c0 + FFN_COL_CHUNK)
        g = jnp.dot(h_buf[...], wg_ref[:, cols], preferred_element_type=F32)
        for l in range(0, FFN_COL_CHUNK, LANES):
            g_buf[(c0 + l) // LANES, SUBLANES:SUBLANES + ts, :] = g[:, l:l + LANES]
        up_buf[:, cols] = jnp.dot(h_buf[...], wu_ref[:, cols], preferred_element_type=F32)
        w0, w1, w2 = (_bcast_rows(cw_ref, k, cols, ROW_CHUNK) for k in range(3))
        bias = _bcast_rows(cb_ref, 0, cols, ROW_CHUNK)
        for r in range(0, ts, ROW_CHUNK):
            gc = (w0 * g_rows(c0, r + SUBLANES - 2) + w1 * g_rows(c0, r + SUBLANES - 1)
                  + w2 * g_rows(c0, r + SUBLANES) + bias)
            act_buf[r:r + ROW_CHUNK, cols] = (gc * _sigmoid(gc) * up_buf[r:r + ROW_CHUNK, cols]).astype(BF16)
    o_ref[0] = x + jnp.dot(act_buf[...], wd_ref[...], preferred_element_type=F32)
    g_buf[:, 0:SUBLANES, :] = g_buf[:, ts:ts + SUBLANES, :]


def _ffn_kernel(x_ref, ng_ref, wg_ref, wu_ref, cw_ref, cb_ref, wd_ref, o_ref,
                *scratch, ts):
    _ffn_body(x_ref[0], ng_ref, wg_ref, wu_ref, cw_ref, cb_ref, wd_ref, o_ref, *scratch, ts)


def _proj_ffn_kernel(x_ref, ya_ref, yb_ref, wo_ref, ng_ref, wg_ref, wu_ref, cw_ref, cb_ref, wd_ref, o_ref,
                     *scratch, ts):
    wa = ya_ref.shape[-1]
    yb = jnp.concatenate([yb_ref[0, c] for c in range(yb_ref.shape[1])], axis=1)
    x = x_ref[0] + jnp.dot(ya_ref[0].astype(BF16), wo_ref[0:wa, :], preferred_element_type=F32)
    x = x + jnp.dot(yb.astype(BF16), wo_ref[wa:, :], preferred_element_type=F32)
    _ffn_body(x, ng_ref, wg_ref, wu_ref, cw_ref, cb_ref, wd_ref, o_ref, *scratch, ts)


def _layer_spec(stacked, layer):
    zeros = (0,) * (stacked.ndim - 1)
    return pl.BlockSpec((None,) + stacked.shape[1:], lambda *_: (layer,) + zeros, pipeline_mode=pl.Buffered(1))


def _conv_ffn(x, layer, ng, wg, wu, cw, cb, wd, mix=None, ts=SEQ_TILE):
    B, S, D = x.shape
    ffn_specs = [
        _const_spec((1, D)),
        _layer_spec(wg, layer),
        _layer_spec(wu, layer),
        _const_spec(cw.shape),
        _const_spec(cb.shape),
        _layer_spec(wd, layer),
    ]
    if mix is None:
        body, ins, specs = _ffn_kernel, (x,), [_tile_spec(D, ts)]
    else:
        ya, yb, wo = mix
        body, ins = _proj_ffn_kernel, (x, ya, yb, wo)
        specs = [_tile_spec(D, ts), _tile_spec(ya.shape[-1], ts), _lane_split_spec(yb.shape[1] * LANES, ts),
                 _const_spec(wo.shape)]
    return pl.pallas_call(
        functools.partial(body, ts=ts),
        grid=(B, S // ts),
        in_specs=specs + ffn_specs,
        out_specs=_tile_spec(D, ts),
        out_shape=jax.ShapeDtypeStruct((B, S, D), F32),
        scratch_shapes=[
            pltpu.VMEM((ts, D), BF16),
            pltpu.VMEM((D_FF // LANES, ts + SUBLANES, LANES), F32),
            pltpu.VMEM((ts, D_FF), F32),
            pltpu.VMEM((ts, D_FF), BF16),
        ],
        compiler_params=pltpu.CompilerParams(
            dimension_semantics=("arbitrary", "arbitrary"), vmem_limit_bytes=VMEM_LIMIT),
        name="conv_ffn" if mix is None else "proj_conv_ffn",
    )(*ins, ng, wg, wu, cw, cb, wd)


def _odd_in_kernel(x_ref, g_ref, win_ref, cw_ref, cb_ref, wa_ref, ba_ref, wx_ref, bx_ref, lam_ref,
                   qg_ref, kg_ref, seg_ref,
                   ylru_ref, qkv0_ref, qkv1_ref, qkv2_ref,
                   h_buf, xin_buf, xc_buf, xcb_buf, ga_buf, gx_buf, gate_buf, t_buf, nq_buf, sc_buf, a_buf, b_buf,
                   al_buf, bl_buf, hin_buf, h_carry, *, ts):
    qkv_refs = (qkv0_ref, qkv1_ref, qkv2_ref)
    @pl.when(pl.program_id(1) == 0)
    def _():
        xin_buf[:, 0:SUBLANES, :] = jnp.zeros((LRU_WIDTH // LANES, SUBLANES, LANES), F32)
        h_carry[...] = jnp.zeros((SUBLANES, LRU_WIDTH), F32)

    all_cols = slice(None)
    h_buf[...] = _rms_bf16(x_ref[0], g_ref[...])

    def proj(c0, width):
        return jnp.dot(h_buf[...], win_ref[:, c0:c0 + width], preferred_element_type=F32)

    def lru_input():
        _store_lanes(xin_buf, SUBLANES, proj(0, LRU_WIDTH))
        for r in range(0, ts, ROW_CHUNK):
            acc = _bcast_rows(cb_ref, 0, all_cols, ROW_CHUNK)
            for k in range(LRU_TAPS):
                acc += (_bcast_rows(cw_ref, k, all_cols, ROW_CHUNK)
                        * _load_lanes(xin_buf, SUBLANES + r - (LRU_TAPS - 1) + k, ROW_CHUNK))
            xc_buf[r:r + ROW_CHUNK, :] = acc
            xcb_buf[r:r + ROW_CHUNK, :] = acc.astype(BF16)
        xin_buf[:, 0:SUBLANES, :] = xin_buf[:, ts:ts + SUBLANES, :]

    def lru_gates():
        for w_ref, buf in ((wa_ref, ga_buf), (wx_ref, gx_buf)):
            for c0 in range(0, LRU_WIDTH, GATE_BLOCK):
                cols = slice(c0, c0 + GATE_BLOCK)
                buf[:, cols] = jnp.dot(xcb_buf[:, cols], w_ref[cols, cols], preferred_element_type=F32)

    def lru_gate_proj():
        gate_buf[...] = proj(LRU_WIDTH, LRU_WIDTH)

    nlam = -lam_ref[...]
    softplus = jnp.maximum(nlam, 0.0) + jnp.log(1.0 + jnp.exp(-jnp.abs(nlam)))
    decay = jnp.broadcast_to(-LRU_C * softplus, (SUBLANES, LRU_WIDTH))
    row = lax.broadcasted_iota(jnp.int32, (SUBLANES, LRU_WIDTH), 0)

    def scan_group(g):
        rows = slice(g * SUBLANES, (g + 1) * SUBLANES)
        rg = _sigmoid(ga_buf[rows, :] + ba_ref[...])
        ig = _sigmoid(gx_buf[rows, :] + bx_ref[...])
        a = jnp.exp(decay * rg)
        b = jnp.sqrt(1.0 - a * a) * (ig * xc_buf[rows, :])
        for s in (1, 2, 4):
            a_prev = jnp.where(row >= s, pltpu.roll(a, s, 0), 1.0)
            b_prev = jnp.where(row >= s, pltpu.roll(b, s, 0), 0.0)
            b = a * b_prev + b
            a = a * a_prev
        a_buf[rows, :] = a
        b_buf[rows, :] = b
        al_buf[rows, :] = jnp.broadcast_to(a[SUBLANES - 1:SUBLANES, :], (SUBLANES, LRU_WIDTH))
        bl_buf[rows, :] = jnp.broadcast_to(b[SUBLANES - 1:SUBLANES, :], (SUBLANES, LRU_WIDTH))

    half = ts // 2
    q0 = 2 * LRU_WIDTH
    qk_rows = 4 * ROW_CHUNK
    norm_args = ((qg_ref, DIL_HEAD_DIM ** -0.5 * math.log2(math.e)), (kg_ref, 1.0))
    v_index = len(norm_args)
    n_split = DIL_OUT // LANES

    def qkv_proj(which, h):
        rows = slice(h * half, (h + 1) * half)
        c0 = q0 + which * DIL_QKV
        z = jnp.dot(h_buf[rows, :], win_ref[:, c0:c0 + DIL_QKV], preferred_element_type=F32)
        if which == v_index:
            for c in range(DIL_QKV // LANES):
                nq_buf[which, c, rows, :] = z[:, c * LANES:(c + 1) * LANES]
        else:
            t_buf[which, rows, :] = z

    def qk_norm(which, h):
        gain_ref, scale = norm_args[which]
        for r in range(h * half, (h + 1) * half, qk_rows):
            for g0 in range(0, DIL_QKV, DIL_OUT):
                t = t_buf[which, r:r + qk_rows, g0:g0 + DIL_OUT]
                ss = jnp.dot((t * t).astype(BF16), seg_ref[...], preferred_element_type=F32)
                gain = _bcast_rows(gain_ref, 0, slice(g0, g0 + DIL_OUT), qk_rows) * scale
                t = t * lax.rsqrt(ss * (1.0 / DIL_HEAD_DIM) + EPS) * gain
                for l in range(0, DIL_OUT, LANES):
                    nq_buf[which, (g0 + l) // LANES, r:r + qk_rows, :] = t[:, l:l + LANES]

    def scatter(which, h):
        for g, (_, d) in enumerate(DIL_PATTERNS):
            n = half // d
            for c in range(n_split):
                if d <= MAX_SUBLANE_STRIDE:
                    for r in range(d):
                        rows = slice(h * half, (h + 1) * half) if d == 1 else pl.ds(h * half + r, n, stride=d)
                        qkv_refs[g][0, which, c, r, h * n:(h + 1) * n, :] = (
                            nq_buf[which, g * n_split + c, rows, :].astype(BF16))
                    continue
                inner = d // MAX_SUBLANE_STRIDE
                for r1 in range(MAX_SUBLANE_STRIDE):
                    rows = pl.ds(h * half + r1, half // MAX_SUBLANE_STRIDE, stride=MAX_SUBLANE_STRIDE)
                    sc_buf[which, h, c, r1] = nq_buf[which, g * n_split + c, rows, :]
                for r in range(d):
                    r2, r1 = divmod(r, MAX_SUBLANE_STRIDE)
                    qkv_refs[g][0, which, c, r, h * n:(h + 1) * n, :] = (
                        sc_buf[which, h, c, r1, pl.ds(r2, n, stride=inner), :].astype(BF16))

    lru_input()
    qkv_proj(0, 0)
    lru_gates()
    qkv_proj(1, 0)
    lru_gate_proj()
    qk_norm(0, 0)
    qkv_proj(v_index, 0)
    scatter(0, 0)
    qkv_proj(0, 1)
    qk_norm(1, 0)
    qkv_proj(1, 1)
    scatter(1, 0)
    scatter(v_index, 0)
    qkv_proj(v_index, 1)
    epilogues = {8: lambda: qk_norm(0, 1), 20: lambda: scatter(0, 1), 32: lambda: qk_norm(1, 1),
                 44: lambda: scatter(1, 1), 56: lambda: scatter(v_index, 1)}
    for g in range(ts // SUBLANES):
        scan_group(g)
        if g in epilogues:
            epilogues[g]()

    hc = h_carry[...]
    for r in range(0, ts, SUBLANES):
        rows = slice(r, r + SUBLANES)
        hin_buf[rows, :] = hc
        hc = al_buf[rows, :] * hc + bl_buf[rows, :]
    h_carry[...] = hc
    for r in range(0, ts, ROW_CHUNK):
        rows = slice(r, r + ROW_CHUNK)
        hs = a_buf[rows, :] * hin_buf[rows, :] + b_buf[rows, :]
        gate = gate_buf[rows, :]
        gelu = 0.5 * gate * (1.0 + jnp.tanh(math.sqrt(2.0 / math.pi) * (gate + 0.044715 * (gate * gate * gate))))
        ylru_ref[0, rows, :] = hs * gelu


def _odd_in(x, g, w_in, cw, cb, wa, ba, wx, bx, lam, qg, kg, seg, ts=SEQ_TILE):
    B, S, D = x.shape
    n_split = DIL_OUT // LANES
    return pl.pallas_call(
        functools.partial(_odd_in_kernel, ts=ts),
        grid=(B, S // ts),
        in_specs=[
            _tile_spec(D, ts),
            _const_spec((1, D)),
            _const_spec(w_in.shape),
            _const_spec(cw.shape),
            _const_spec(cb.shape),
            _const_spec(wa.shape),
            _const_spec(ba.shape),
            _const_spec(wx.shape),
            _const_spec(bx.shape),
            _const_spec((1, LRU_WIDTH)),
            _const_spec(qg.shape),
            _const_spec(kg.shape),
            _const_spec(seg.shape),
        ],
        out_specs=[_tile_spec(LRU_WIDTH, ts)] + [
            pl.BlockSpec((1, 3, n_split, d, ts // d, LANES), lambda b, j: (b, 0, 0, 0, j, 0))
            for _, d in DIL_PATTERNS],
        out_shape=[jax.ShapeDtypeStruct((B, S, LRU_WIDTH), F32)] + [
            jax.ShapeDtypeStruct((B, 3, n_split, d, S // d, LANES), BF16) for _, d in DIL_PATTERNS],
        scratch_shapes=[
            pltpu.VMEM((ts, D), BF16),
            pltpu.VMEM((LRU_WIDTH // LANES, ts + SUBLANES, LANES), F32),
            pltpu.VMEM((ts, LRU_WIDTH), F32),
            pltpu.VMEM((ts, LRU_WIDTH), BF16),
            pltpu.VMEM((ts, LRU_WIDTH), F32),
            pltpu.VMEM((ts, LRU_WIDTH), F32),
            pltpu.VMEM((ts, LRU_WIDTH), F32),
            pltpu.VMEM((2, ts, DIL_QKV), F32),
            pltpu.VMEM((3, DIL_QKV // LANES, ts, LANES), F32),
            pltpu.VMEM((3, 2, n_split, MAX_SUBLANE_STRIDE, ts // 2 // MAX_SUBLANE_STRIDE, LANES), F32),
            pltpu.VMEM((ts, LRU_WIDTH), F32),
            pltpu.VMEM((ts, LRU_WIDTH), F32),
            pltpu.VMEM((ts, LRU_WIDTH), F32),
            pltpu.VMEM((ts, LRU_WIDTH), F32),
            pltpu.VMEM((ts, LRU_WIDTH), F32),
            pltpu.VMEM((SUBLANES, LRU_WIDTH), F32),
        ],
        compiler_params=pltpu.CompilerParams(
            dimension_semantics=("arbitrary", "arbitrary"), vmem_limit_bytes=VMEM_LIMIT),
        name="odd_in_lru",
    )(x, g, w_in, cw, cb, wa, ba, wx, bx, lam, qg, kg, seg)


def _attn_kernel(*refs, dilation, prev_dilation, last, seq):
    qkv_ref, refs = refs[0], refs[1:]
    if prev_dilation is not None:
        (prev_o_ref, prev_lse_ref), refs = refs[:2], refs[2:]
    if last:
        o_ref, bias_buf = refs
    else:
        o_ref, lse_ref, bias_buf = refs
    blk = ATTN_BLOCK
    nb = seq // dilation // blk
    n_split = DIL_OUT // LANES
    qi = lax.broadcasted_iota(jnp.int32, (DIL_HEADS * blk, 2 * blk), 0) % blk
    kj = lax.broadcasted_iota(jnp.int32, (DIL_HEADS * blk, 2 * blk), 1)
    for t in range(2):
        steps = qi - kj + t * blk
        bias_buf[t] = jnp.where((steps >= 0) & (steps <= blk), 0.0, -jnp.inf).astype(F32)

    lane_head = lax.broadcasted_iota(jnp.int32, (1, DIL_OUT), 1) // DIL_HEAD_DIM
    is_head = [lane_head == h for h in range(DIL_HEADS)]

    def fold(t):
        out = t[(DIL_HEADS - 1) * blk:DIL_HEADS * blk]
        for h in range(DIL_HEADS - 2, -1, -1):
            out = jnp.where(is_head[h], t[h * blk:(h + 1) * blk], out)
        return out

    def load(which, res, rows):
        return jnp.concatenate([qkv_ref[0, which, c, res, rows, :] for c in range(n_split)], axis=1)

    def body(idx, carry):
        res = idx // nb
        b = idx % nb
        kb = jnp.maximum(b - 1, 0)
        q_rows = pl.ds(pl.multiple_of(b * blk, blk), blk)
        k_rows = pl.ds(pl.multiple_of(kb * blk, blk), 2 * blk)
        qb = load(0, res, q_rows)
        zero = jnp.zeros_like(qb)
        qs = jnp.concatenate([jnp.where(is_head[h], qb, zero) for h in range(DIL_HEADS)], axis=0)
        s = lax.dot_general(qs, load(1, res, k_rows), (((1,), (1,)), ((), ())), preferred_element_type=F32)
        s = s + bias_buf[jnp.minimum(b, 1)]
        m = jnp.max(s, axis=-1, keepdims=True)
        p = jnp.exp2(s - m)
        l = jnp.sum(p, axis=-1, keepdims=True)
        o = jnp.dot(p.astype(BF16), load(2, res, k_rows), preferred_element_type=F32)
        o_tok, m_tok, l_tok = fold(o), fold(m), fold(l)
        o_tok = o_tok / l_tok
        lse = m_tok + jnp.log2(l_tok)
        if prev_dilation is not None:
            step = dilation // prev_dilation
            p_rows = pl.ds(b * (blk * step) + res // prev_dilation, blk, stride=step)
            p_res = res % prev_dilation
            prev_o = jnp.concatenate([prev_o_ref[0, c, p_res, p_rows, :] for c in range(n_split)], axis=1)
            prev_lse = jnp.concatenate([prev_lse_ref[0, c, p_res, p_rows, :] for c in range(n_split)], axis=1)
            top = jnp.maximum(prev_lse, lse)
            w_prev = jnp.exp2(prev_lse - top)
            w_new = jnp.exp2(lse - top)
            den = w_prev + w_new
            o_tok = (w_prev * prev_o + w_new * o_tok) / den
            lse = top + jnp.log2(den)
        if last:
            seq_rows = pl.ds(b * (blk * dilation) + res, blk, stride=dilation)
            for c in range(n_split):
                o_ref[0, c, seq_rows, :] = o_tok[:, c * LANES:(c + 1) * LANES]
        else:
            for c in range(n_split):
                o_ref[0, c, res, q_rows, :] = o_tok[:, c * LANES:(c + 1) * LANES]
                lse_ref[0, c, res, q_rows, :] = lse[:, c * LANES:(c + 1) * LANES]
        return carry

    lax.fori_loop(0, dilation * nb, body, 0, unroll=16)


def _dilated_attention(qkv_groups):
    def batch_block(shape):
        return pl.BlockSpec((1,) + shape[1:], lambda b: (b,) + (0,) * (len(shape) - 1))

    merged, prev_d = (), None
    for g, qkv in enumerate(qkv_groups):
        B, _, n_split, d, L, _ = qkv.shape
        assert DIL_PATTERNS[g] == (ATTN_BLOCK * d, d)
        last = g == len(qkv_groups) - 1
        if last:
            out_shape = [jax.ShapeDtypeStruct((B, n_split, d * L, LANES), F32)]
        else:
            out_shape = [jax.ShapeDtypeStruct((B, n_split, d, L, LANES), F32)] * 2
        ins = (qkv,) + tuple(merged)
        merged = pl.pallas_call(
            functools.partial(_attn_kernel, dilation=d, prev_dilation=prev_d, last=last, seq=d * L),
            grid=(B,),
            in_specs=[batch_block(a.shape) for a in ins],
            out_specs=[batch_block(s.shape) for s in out_shape],
            out_shape=out_shape,
            scratch_shapes=[pltpu.VMEM((2, DIL_HEADS * ATTN_BLOCK, 2 * ATTN_BLOCK), F32)],
            compiler_params=pltpu.CompilerParams(
                dimension_semantics=("arbitrary",), vmem_limit_bytes=VMEM_LIMIT),
            name=f"dilated_attention_d{d}",
        )(*ins)
        prev_d = d
    return merged[0]


def _block_diag(w):
    H, n, _ = w.shape
    eye = jnp.eye(H, dtype=w.dtype)
    return (eye[:, None, :, None] * w[:, :, None, :]).reshape(H * n, H * n)


def kernel(x, mix_norm_g, ffn_norm_g, ev_w_in, ev_sc_conv_w, ev_cf_conv_w, ev_cf_conv_b, ev_cf_ln_g, ev_cf_ln_b, ev_w_out, od_w_in, od_lru_conv_w, od_lru_conv_b, od_lru_wa, od_lru_ba, od_lru_wx, od_lru_bx, od_lru_lam, od_q_norm_g, od_k_norm_g, od_w_out, ffn_w_gate, ffn_w_up, ffn_conv_w, ffn_conv_b, ffn_w_down):
    row = lambda p: p.reshape(1, -1)
    bf = lambda w: w.astype(BF16)
    n_heads = DIL_QKV // DIL_HEAD_DIM
    head_of = jnp.arange(DIL_OUT) // DIL_HEAD_DIM
    seg = (head_of[:, None] == head_of[None, :]).astype(BF16)

    ffn_wg, ffn_wu, ffn_wd = bf(ffn_w_gate), bf(ffn_w_up), bf(ffn_w_down)

    def ffn(x, layer, mix=None):
        return _conv_ffn(x, layer, row(ffn_norm_g[layer]), ffn_wg, ffn_wu,
                         _rep8(ffn_conv_w[layer]), _rep8(ffn_conv_b[layer]), ffn_wd, mix=mix)

    x = _even_mixer(x, row(mix_norm_g[0]), bf(ev_w_in[0]), _rep8(ev_sc_conv_w[0]), _rep8(ev_cf_conv_w[0]),
                    _rep8(ev_cf_conv_b[0]), _rep8(ev_cf_ln_g[0]), _rep8(ev_cf_ln_b[0]), bf(ev_w_out[0]))
    x = ffn(x, 0)
    y_lru, *qkv_groups = _odd_in(
        x, row(mix_norm_g[1]), bf(od_w_in[0]), _rep8(od_lru_conv_w[0]), _rep8(od_lru_conv_b[0]),
        bf(_block_diag(od_lru_wa[0])), _rep8(od_lru_ba[0]), bf(_block_diag(od_lru_wx[0])), _rep8(od_lru_bx[0]),
        row(od_lru_lam[0]), _rep8(jnp.tile(od_q_norm_g[0], n_heads)), _rep8(jnp.tile(od_k_norm_g[0], n_heads)), seg)
    y_att = _dilated_attention(qkv_groups)
    return ffn(x, 1, mix=(y_lru, y_att, bf(od_w_out[0])))
```
